```python
import jax, jax.numpy as jnp
from jax import lax
import numpy as np

D_MODEL = 1024
BATCH = 8
SEQ = 8192
DEPTH = 1

GRID_W = 64
CTX_LEN = 256

NA_HEADS = 8
NA_HEAD_DIM = 64
NA_WIN_ROWS = 8
NA_WIN_COLS = 16
NA_WIDTH = NA_HEADS * NA_HEAD_DIM

GLA_HEADS = 4
GLA_KEY_DIM = 64
GLA_VAL_DIM = 128
GLA_QK_WIDTH = GLA_HEADS * GLA_KEY_DIM
GLA_V_WIDTH = GLA_HEADS * GLA_VAL_DIM
GLA_GATE_RANK = 16
GLA_GATE_TAU = 16.0
GLA_CHUNK = 64

MIX_WIDTH = NA_WIDTH + GLA_V_WIDTH
IN_SPLITS = (NA_WIDTH, NA_WIDTH, NA_WIDTH, GLA_QK_WIDTH, GLA_QK_WIDTH,
             GLA_V_WIDTH, GLA_V_WIDTH, GLA_GATE_RANK, GLA_GATE_RANK)
IN_WIDTH = 3 * NA_WIDTH + 2 * GLA_QK_WIDTH + 2 * GLA_V_WIDTH + 2 * GLA_GATE_RANK

FFN_HIDDEN = ((8 * D_MODEL + 3 * 256 - 1) // (3 * 256)) * 256
ROPE_BASE = 10000.0
NORM_EPS = 1e-6

kernel_name = "hybrid_na_gla_dit_layer"


def rmsnorm(x, g):
    xf = x.astype(jnp.float32)
    y = xf * lax.rsqrt(jnp.mean(xf * xf, axis=-1, keepdims=True) + NORM_EPS)
    return (y * g.astype(jnp.float32)).astype(x.dtype)


def modulate(x, shift, scale):
    return x * (1 + scale) + shift


def ada_mod(cond, w_mod, b_mod):
    return jnp.split(jax.nn.silu(cond) @ w_mod + b_mod, 6, axis=-1)


def split_proj(p):
    offs = [int(o) for o in np.cumsum(IN_SPLITS)[:-1]]
    return jnp.split(p, offs, axis=-1)


def to_heads(t, n_heads):
    b, l, _ = t.shape
    return t.reshape(b, l, n_heads, -1).transpose(0, 2, 1, 3)


def from_heads(t):
    b, h, l, d = t.shape
    return t.transpose(0, 2, 1, 3).reshape(b, l, h * d)


def rope_1d(x, pos):
    half = x.shape[-1] // 2
    inv = ROPE_BASE ** (-jnp.arange(half, dtype=jnp.float32) / half)
    ang = pos.astype(jnp.float32)[:, None] * inv[None, :]
    cos, sin = jnp.cos(ang).astype(x.dtype), jnp.sin(ang).astype(x.dtype)
    x1, x2 = x[..., :half], x[..., half:]
    return jnp.concatenate([x1 * cos - x2 * sin, x1 * sin + x2 * cos], axis=-1)


def axial_rope(x):
    l = x.shape[2]
    t = jnp.arange(l)
    half = x.shape[-1] // 2
    return jnp.concatenate([rope_1d(x[..., :half], t // GRID_W),
                            rope_1d(x[..., half:], t % GRID_W)], axis=-1)


def neighbourhood_attention(q, k, v, k_ctx, v_ctx, rpb):
    b, h, l, dh = q.shape
    rows = l // GRID_W
    kr = min(NA_WIN_ROWS, rows)
    kw = NA_WIN_COLS
    grid = lambda t: t.reshape(b, h, rows, GRID_W, dh)
    qg, kg, vg = grid(q * dh ** -0.5), grid(k), grid(v)
    cols = np.arange(GRID_W)
    col_start = np.clip(cols - kw // 2, 0, GRID_W - kw)
    col_idx = col_start[:, None] + np.arange(kw)[None, :]
    col_off = col_idx - cols[:, None] + (kw - 1)
    rpb_cols = rpb[:, :, col_off]

    def one_row(r):
        rs = jnp.clip(r - kr // 2, 0, rows - kr)
        q_r = lax.dynamic_index_in_dim(qg, r, axis=2, keepdims=False)
        k_band = lax.dynamic_slice_in_dim(kg, rs, kr, axis=2)
        v_band = lax.dynamic_slice_in_dim(vg, rs, kr, axis=2)
        k_win = k_band[:, :, :, col_idx]
        v_win = v_band[:, :, :, col_idx]
        row_off = rs + jnp.arange(kr) - r + (NA_WIN_ROWS - 1)
        bias = jnp.take(rpb_cols, row_off, axis=1).transpose(0, 2, 1, 3)
        s_loc = (jnp.einsum('bhcd,bhrcwd->bhcrw', q_r, k_win).astype(jnp.float32)
                 + bias[None].astype(jnp.float32))
        s_ctx = jnp.einsum('bhcd,bhnd->bhcn', q_r, k_ctx).astype(jnp.float32)
        s = jnp.concatenate([s_loc.reshape(b, h, GRID_W, kr * kw), s_ctx], axis=-1)
        p = jax.nn.softmax(s, axis=-1).astype(v.dtype)
        p_loc = p[..., :kr * kw].reshape(b, h, GRID_W, kr, kw)
        p_ctx = p[..., kr * kw:]
        return (jnp.einsum('bhcrw,bhrcwd->bhcd', p_loc, v_win)
                + jnp.einsum('bhcn,bhnd->bhcd', p_ctx, v_ctx))

    out = lax.map(one_row, jnp.arange(rows))
    return out.transpose(1, 2, 0, 3, 4).reshape(b, h, l, dh)


def ctx_self_attention(q, k, v):
    s = jnp.einsum('bhqd,bhkd->bhqk', q * q.shape[-1] ** -0.5, k).astype(jnp.float32)
    return jnp.einsum('bhqk,bhkd->bhqd', jax.nn.softmax(s, axis=-1).astype(v.dtype), v)


def gla_chunked(q, k, v, logg, s0):
    b_, h, l, _ = q.shape
    dv = v.shape[-1]
    n = l // GLA_CHUNK
    ch = lambda t: t.astype(jnp.float32).reshape(b_, h, n, GLA_CHUNK, t.shape[-1])
    q, k, v, logg = ch(q), ch(k), ch(v), ch(logg)
    bcum = jnp.cumsum(logg, axis=3)
    b_end = bcum[:, :, :, -1:, :]
    q_dec = q * jnp.exp(bcum)
    k_inv = k * jnp.exp(-bcum)
    k_end = k * jnp.exp(b_end - bcum)
    lower = jnp.tril(jnp.ones((GLA_CHUNK, GLA_CHUNK), dtype=bool))
    attn = jnp.where(lower, jnp.einsum('bhncd,bhnsd->bhncs', q_dec, k_inv), 0.0)
    o_intra = jnp.einsum('bhncs,bhnse->bhnce', attn, v)
    kv_chunk = jnp.einsum('bhncd,bhnce->bhnde', k_end, v)
    decay = jnp.exp(b_end[:, :, :, 0, :])

    def step(s, inp):
        d, kv = inp
        return d[..., None] * s + kv, s

    _, s_prev = lax.scan(step, s0.astype(jnp.float32),
                         (jnp.moveaxis(decay, 2, 0), jnp.moveaxis(kv_chunk, 2, 0)))
    s_prev = jnp.moveaxis(s_prev, 0, 2)
    o_inter = jnp.einsum('bhncd,bhnde->bhnce', q_dec, s_prev)
    return (o_intra + o_inter).reshape(b_, h, l, dv)


def gla_final_state(k, v, logg):
    k, v, logg = k.astype(jnp.float32), v.astype(jnp.float32), logg.astype(jnp.float32)
    bcum = jnp.cumsum(logg, axis=2)
    return jnp.einsum('bhtd,bhte->bhde', k * jnp.exp(bcum[:, :, -1:] - bcum), v)


def gla_output(o, r, gain):
    o = o * lax.rsqrt(jnp.mean(o * o, axis=-1, keepdims=True) + NORM_EPS)
    y = from_heads(o) * gain.astype(jnp.float32) * jax.nn.silu(r.astype(jnp.float32))
    return y.astype(r.dtype)


def project(h, w_in, wa2_f, ba_f, wa2_b, ba_b):
    na_q, na_k, na_v, gq, gk, gv, gr, af, ab = split_proj(h @ w_in)
    logf = jax.nn.log_sigmoid((af @ wa2_f + ba_f).astype(jnp.float32)) / GLA_GATE_TAU
    logb = jax.nn.log_sigmoid((ab @ wa2_b + ba_b).astype(jnp.float32)) / GLA_GATE_TAU
    return (to_heads(na_q, NA_HEADS), to_heads(na_k, NA_HEADS), to_heads(na_v, NA_HEADS),
            to_heads(gq, GLA_HEADS), to_heads(gk, GLA_HEADS), to_heads(gv, GLA_HEADS), gr,
            to_heads(logf, GLA_HEADS), to_heads(logb, GLA_HEADS))


def mixer(h, hc, w_in, na_rpb, wa2_f, ba_f, wa2_b, ba_b, gla_norm, w_out, with_ctx_out):
    nq, nk, nv, gq, gk, gv, gr, lf, lb = project(h, w_in, wa2_f, ba_f, wa2_b, ba_b)
    cnq, cnk, cnv, cgq, cgk, cgv, cgr, clf, clb = project(hc, w_in, wa2_f, ba_f, wa2_b, ba_b)
    flip = lambda t: jnp.flip(t, axis=2)
    na_out = from_heads(neighbourhood_attention(nq, nk, nv, cnk, cnv, na_rpb))
    gq = axial_rope(gq) * GLA_KEY_DIM ** -0.5
    gk = axial_rope(gk)
    s_f = gla_final_state(cgk, cgv, clf)
    s_b = gla_final_state(flip(cgk), flip(cgv), flip(clb))
    o = (gla_chunked(gq, gk, gv, lf, s_f)
         + flip(gla_chunked(flip(gq), flip(gk), flip(gv), flip(lb), s_b)))
    gla_out = gla_output(o, gr, gla_norm)
    y = jnp.concatenate([na_out, gla_out.astype(na_out.dtype)], axis=-1) @ w_out
    if not with_ctx_out:
        return y, None
    cgq = cgq * GLA_KEY_DIM ** -0.5
    zero = jnp.zeros_like(s_f)
    co = (gla_chunked(cgq, cgk, cgv, clf, zero)
          + flip(gla_chunked(flip(cgq), flip(cgk), flip(cgv), flip(clb), zero)))
    na_c = from_heads(ctx_self_attention(cnq, cnk, cnv))
    yc = jnp.concatenate([na_c, gla_output(co, cgr, gla_norm).astype(na_c.dtype)], axis=-1) @ w_out
    return y, yc


def swiglu(h, w_gate_up, w_down):
    g, u = jnp.split(h @ w_gate_up, 2, axis=-1)
    return (jax.nn.silu(g) * u) @ w_down


def setup_inputs(seed: int = 0) -> dict:
    key = jax.random.key(seed)
    ks = jax.random.split(key, 20)
    nrm = lambda k, shape, scale: jax.random.normal(k, shape, jnp.float32) * scale
    gain = lambda k, width: 1.0 + nrm(k, (DEPTH, width), 0.05)
    return {
        "x": nrm(ks[0], (BATCH, SEQ, D_MODEL), 1.0),
        "c": nrm(ks[1], (BATCH, D_MODEL), 1.0),
        "ctx": nrm(ks[2], (BATCH, CTX_LEN, D_MODEL), 1.0),
        "c_ctx": nrm(ks[3], (D_MODEL,), 1.0),
        "w_mod": nrm(ks[4], (DEPTH, D_MODEL, 6 * D_MODEL), 0.5 * D_MODEL ** -0.5),
        "b_mod": nrm(ks[5], (DEPTH, 6 * D_MODEL), 0.02),
        "norm_pre_mix": gain(ks[6], D_MODEL),
        "norm_post_mix": gain(ks[7], D_MODEL),
        "norm_pre_ffn": gain(ks[8], D_MODEL),
        "norm_post_ffn": gain(ks[9], D_MODEL),
        "w_in": nrm(ks[10], (DEPTH, D_MODEL, IN_WIDTH), D_MODEL ** -0.5),
        "na_rpb": nrm(ks[11], (DEPTH, NA_HEADS, 2 * NA_WIN_ROWS - 1, 2 * NA_WIN_COLS - 1), 0.1),
        "gla_wa2_f": nrm(ks[12], (DEPTH, GLA_GATE_RANK, GLA_QK_WIDTH), GLA_GATE_RANK ** -0.5),
        "gla_ba_f": nrm(ks[13], (DEPTH, GLA_QK_WIDTH), 0.1),
        "gla_wa2_b": nrm(ks[14], (DEPTH, GLA_GATE_RANK, GLA_QK_WIDTH), GLA_GATE_RANK ** -0.5),
        "gla_ba_b": nrm(ks[15], (DEPTH, GLA_QK_WIDTH), 0.1),
        "gla_norm": gain(ks[16], GLA_V_WIDTH),
        "w_out": nrm(ks[17], (DEPTH, MIX_WIDTH, D_MODEL), MIX_WIDTH ** -0.5),
        "w_gate_up": nrm(ks[18], (DEPTH, D_MODEL, 2 * FFN_HIDDEN), D_MODEL ** -0.5),
        "w_down": nrm(ks[19], (DEPTH, FFN_HIDDEN, D_MODEL), FFN_HIDDEN ** -0.5),
    }


def reference(x, c, ctx, c_ctx, w_mod, b_mod, norm_pre_mix, norm_post_mix, norm_pre_ffn,
              norm_post_ffn, w_in, na_rpb, gla_wa2_f, gla_ba_f, gla_wa2_b, gla_ba_b,
              gla_norm, w_out, w_gate_up, w_down):
    for i in range(DEPTH):
        last = i == DEPTH - 1
        sh1, sc1, gt1, sh2, sc2, gt2 = [m[:, None, :] for m in ada_mod(c, w_mod[i], b_mod[i])]
        csh1, csc1, cgt1, csh2, csc2, cgt2 = ada_mod(c_ctx, w_mod[i], b_mod[i])
        h = modulate(rmsnorm(x, norm_pre_mix[i]), sh1, sc1)
        hc = modulate(rmsnorm(ctx, norm_pre_mix[i]), csh1, csc1)
        y, yc = mixer(h, hc, w_in[i], na_rpb[i], gla_wa2_f[i], gla_ba_f[i], gla_wa2_b[i],
                      gla_ba_b[i], gla_norm[i], w_out[i], not last)
        x = x + gt1 * rmsnorm(y, norm_post_mix[i])
        h = modulate(rmsnorm(x, norm_pre_ffn[i]), sh2, sc2)
        x = x + gt2 * rmsnorm(swiglu(h, w_gate_up[i], w_down[i]), norm_post_ffn[i])
        if not last:
            ctx = ctx + cgt1 * rmsnorm(yc, norm_post_mix[i])
            hc = modulate(rmsnorm(ctx, norm_pre_ffn[i]), csh2, csc2)
            ctx = ctx + cgt2 * rmsnorm(swiglu(hc, w_gate_up[i], w_down[i]), norm_post_ffn[i])
    return x
```

```python
import functools

import numpy as np
import jax
import jax.numpy as jnp
from jax import lax
from jax.experimental import pallas as pl
from jax.experimental.pallas import tpu as pltpu

F32 = jnp.float32
BF16 = jnp.bfloat16

GRID_W = 64
NA_HEADS = 8
NA_HEAD_DIM = 64
NA_WIN_ROWS = 8
NA_WIN_COLS = 16
GLA_HEADS = 4
GLA_KEY_DIM = 64
GLA_VAL_DIM = 128
GLA_GATE_RANK = 16
GLA_GATE_TAU = 16.0
GLA_CHUNK = 64
ROPE_BASE = 10000.0
NORM_EPS = 1e-6

NA_WIDTH = NA_HEADS * NA_HEAD_DIM
GLA_QK_WIDTH = GLA_HEADS * GLA_KEY_DIM
GLA_V_WIDTH = GLA_HEADS * GLA_VAL_DIM
MAIN_WIDTH = 3 * NA_WIDTH + 2 * GLA_QK_WIDTH + 2 * GLA_V_WIDTH

LANES = 128
MASK_VALUE = -1e30
VMEM_LIMIT_BYTES = 56 * 1024 * 1024

NT_DIMS = (((1,), (1,)), ((), ()))
TN_DIMS = (((0,), (0,)), ((), ()))


def _dot(a, b):
    return jnp.dot(a, b, preferred_element_type=F32)


def _dot_nt(a, b):
    return lax.dot_general(a, b, NT_DIMS, preferred_element_type=F32)


def _dot_tn(a, b):
    return lax.dot_general(a, b, TN_DIMS, preferred_element_type=F32)


def _split_bf16(a):
    hi = a.astype(BF16)
    lo = (a - hi.astype(F32)).astype(BF16)
    return hi, lo


def _sigmoid(x):
    return 1.0 / (1.0 + jnp.exp(-x))


def _rms(x):
    return x * lax.rsqrt(jnp.mean(x * x, axis=-1, keepdims=True) + NORM_EPS)


def _params(*semantics):
    return pltpu.CompilerParams(dimension_semantics=semantics, vmem_limit_bytes=VMEM_LIMIT_BYTES)


def _resident(shape):
    zeros = (0,) * len(shape)
    return pl.BlockSpec(shape, lambda *_: zeros)


def _mod_kernel(c_ref, w_ref, b_ref, o_ref):
    c = c_ref[...]
    s_hi, s_lo = _split_bf16(c * _sigmoid(c))
    w_hi, w_lo = _split_bf16(w_ref[...])
    o_ref[...] = _dot(s_hi, w_hi) + (_dot(s_hi, w_lo) + _dot(s_lo, w_hi)) + b_ref[...]


def _adaln_mod(cond, w_mod, b_mod):
    r, d = cond.shape
    n = w_mod.shape[1]
    tn = 1024
    return pl.pallas_call(
        _mod_kernel,
        grid=(n // tn,),
        in_specs=[pl.BlockSpec((r, d), lambda j: (0, 0)),
                  pl.BlockSpec((d, tn), lambda j: (0, j)),
                  pl.BlockSpec((1, tn), lambda j: (0, j))],
        out_specs=pl.BlockSpec((r, tn), lambda j: (0, j)),
        out_shape=jax.ShapeDtypeStruct((r, n), F32),
        compiler_params=_params("arbitrary"),
        name="adaln_mod",
    )(cond, w_mod, b_mod)


def _rope_tables(seq_len, rotate):
    if not rotate:
        return jnp.ones((seq_len, LANES), F32), jnp.zeros((seq_len, LANES), F32)
    quarter = GLA_KEY_DIM // 4
    t = jnp.arange(seq_len)
    inv = ROPE_BASE ** (-jnp.arange(quarter, dtype=F32) / quarter)
    ang_row = (t // GRID_W).astype(F32)[:, None] * inv[None, :]
    ang_col = (t % GRID_W).astype(F32)[:, None] * inv[None, :]
    cos = jnp.concatenate([jnp.cos(ang_row)] * 2 + [jnp.cos(ang_col)] * 2, axis=-1)
    sin = jnp.concatenate([-jnp.sin(ang_row), jnp.sin(ang_row), -jnp.sin(ang_col), jnp.sin(ang_col)], axis=-1)
    reps = LANES // GLA_KEY_DIM
    return jnp.tile(cos, (1, reps)), jnp.tile(sin, (1, reps))


def _in_proj_kernel(x_ref, mod_ref, g_ref, w_ref, wg_ref, wa_ref, ba_ref, cos_ref, sin_ref,
                    naq_ref, nak_ref, nav_ref, gq_ref, gk_ref, gv_ref, gr_ref, lf_ref, lb_ref):
    x = x_ref[...]
    shift, scale = mod_ref[0:1, :], mod_ref[1:2, :]
    h = (_rms(x) * g_ref[...]) * (1.0 + scale) + shift
    hb = h.astype(BF16)

    def proj(c0, width):
        return _dot(hb, w_ref[:, c0:c0 + width])

    naq_ref[...] = (proj(0, NA_WIDTH) * NA_HEAD_DIM ** -0.5).astype(BF16)
    nak_ref[...] = proj(NA_WIDTH, NA_WIDTH).astype(BF16)
    nav_ref[...] = proj(2 * NA_WIDTH, NA_WIDTH).astype(BF16)

    cos, sin = cos_ref[...], sin_ref[...]
    quarter = GLA_KEY_DIM // 4
    lane = lax.broadcasted_iota(jnp.int32, cos.shape, 1)
    first_half = (lane & quarter) == 0

    def rope(t):
        partner = jnp.where(first_half, pltpu.roll(t, LANES - quarter, 1), pltpu.roll(t, quarter, 1))
        return t * cos + partner * sin

    c_gq = 3 * NA_WIDTH
    c_gk = c_gq + GLA_QK_WIDTH
    for j in range(GLA_QK_WIDTH // LANES):
        lanes = slice(j * LANES, (j + 1) * LANES)
        gq_ref[:, lanes] = (rope(proj(c_gq + j * LANES, LANES)) * GLA_KEY_DIM ** -0.5).astype(BF16)
        gk_ref[:, lanes] = rope(proj(c_gk + j * LANES, LANES)).astype(BF16)

    c_gv = c_gk + GLA_QK_WIDTH
    gv_ref[...] = proj(c_gv, GLA_V_WIDTH).astype(BF16)
    gr_ref[...] = proj(c_gv + GLA_V_WIDTH, GLA_V_WIDTH).astype(BF16)

    low_rank = _dot(hb, wg_ref[...]).astype(BF16)
    z = _dot(low_rank, wa_ref[...]) + ba_ref[...]
    log_gate = (jnp.minimum(z, 0.0) - jnp.log(1.0 + jnp.exp(-jnp.abs(z)))) * (1.0 / GLA_GATE_TAU)
    lf_ref[...] = log_gate[:, :GLA_QK_WIDTH]
    lb_ref[...] = log_gate[:, GLA_QK_WIDTH:]


def _in_proj(x, mods, gain, w_main, w_gate, w_a2, b_a2, rotate, tm):
    b, l, d = x.shape
    per_batch = mods.shape[0] > 1
    cos, sin = _rope_tables(l, rotate)
    tok = lambda width: pl.BlockSpec((None, tm, width), lambda bi, i: (bi, i, 0))
    out_widths = [NA_WIDTH] * 3 + [GLA_QK_WIDTH] * 2 + [GLA_V_WIDTH] * 2 + [GLA_QK_WIDTH] * 2
    out_dtypes = [BF16] * 7 + [F32] * 2
    return pl.pallas_call(
        _in_proj_kernel,
        grid=(b, l // tm),
        in_specs=[tok(d),
                  pl.BlockSpec((None,) + mods.shape[1:], lambda bi, i: (bi if per_batch else 0, 0, 0)),
                  _resident(gain.shape), _resident(w_main.shape), _resident(w_gate.shape),
                  _resident(w_a2.shape), _resident(b_a2.shape),
                  pl.BlockSpec((tm, LANES), lambda bi, i: (i, 0)),
                  pl.BlockSpec((tm, LANES), lambda bi, i: (i, 0))],
        out_specs=[tok(w) for w in out_widths],
        out_shape=[jax.ShapeDtypeStruct((b, l, w), dt) for w, dt in zip(out_widths, out_dtypes)],
        compiler_params=_params("parallel", "parallel"),
        name="in_proj",
    )(x, mods, gain, w_main, w_gate, w_a2, b_a2, cos, sin)


def _gla_kernel(*refs, reverse, finalize, chunks):
    if finalize:
        q_ref, k_ref, lg_ref, v_ref, s0_ref, prev_ref, gr_ref, gain_ref, out_ref, sfin_ref, st_ref = refs
    else:
        q_ref, k_ref, lg_ref, v_ref, s0_ref, out_ref, sfin_ref, st_ref = refs
    step = pl.program_id(2)

    @pl.when(step == 0)
    def _():
        st_ref[...] = s0_ref[...]

    c = GLA_CHUNK
    row = lax.broadcasted_iota(jnp.int32, (c, c), 0)
    col = lax.broadcasted_iota(jnp.int32, (c, c), 1)
    tri = (row <= col) if reverse else (row >= col)
    tri_b = tri.astype(BF16)
    lane = lax.broadcasted_iota(jnp.int32, (c, LANES), 1)
    head_lanes = [lane < GLA_KEY_DIM, lane >= GLA_KEY_DIM]
    end_row = 0 if reverse else c - 1

    def chunk(ci, carry):
        idx = (chunks - 1 - ci) if reverse else ci
        rows = pl.ds(pl.multiple_of(idx * c, c), c)
        lg_hi, lg_lo = _split_bf16(lg_ref[rows, :])
        bcum = _dot(tri_b, lg_hi) + _dot(tri_b, lg_lo)
        b_end = bcum[end_row:end_row + 1, :]
        q = q_ref[rows, :].astype(F32)
        k = k_ref[rows, :].astype(F32)
        q_dec = q * jnp.exp(bcum)
        k_inv = (k * jnp.exp(-bcum)).astype(BF16)
        k_end = (k * jnp.exp(b_end - bcum)).astype(BF16)
        decay = jnp.exp(b_end)
        for h in range(2):
            vals = slice(h * GLA_VAL_DIM, (h + 1) * GLA_VAL_DIM)
            q_h = jnp.where(head_lanes[h], q_dec, 0.0).astype(BF16)
            attn = jnp.where(tri, _dot_nt(q_h, k_inv), 0.0).astype(BF16)
            v_h = v_ref[rows, vals]
            st = st_ref[h]
            o = _dot(attn, v_h) + _dot_nt(q_h, st.astype(BF16))
            st_ref[h] = st * decay + _dot_tn(v_h, k_end)
            if finalize:
                o = _rms(o + prev_ref[rows, vals])
                gate = gr_ref[rows, vals].astype(F32)
                out_ref[rows, vals] = (o * gain_ref[:, vals] * (gate * _sigmoid(gate))).astype(out_ref.dtype)
            else:
                out_ref[rows, vals] = o
        return carry

    lax.fori_loop(0, chunks, chunk, 0)

    @pl.when(step == pl.num_programs(2) - 1)
    def _():
        sfin_ref[...] = st_ref[...]


def _gla_scan(q, k, lg, v, s0, *, reverse, tb, prev=None, gate=None, gain=None):
    b, l, _ = q.shape
    finalize = prev is not None
    nb = l // tb
    pairs = GLA_HEADS // 2
    blk = (lambda i: nb - 1 - i) if reverse else (lambda i: i)
    qk_spec = pl.BlockSpec((None, tb, LANES), lambda bi, p, i: (bi, blk(i), p))
    v_spec = pl.BlockSpec((None, tb, 2 * GLA_VAL_DIM), lambda bi, p, i: (bi, blk(i), p))
    st_spec = pl.BlockSpec((None, 2, GLA_VAL_DIM, LANES), lambda bi, p, i: (bi, p, 0, 0))
    in_specs = [qk_spec, qk_spec, qk_spec, v_spec, st_spec]
    args = [q, k, lg, v, s0]
    if finalize:
        in_specs += [v_spec, v_spec, pl.BlockSpec((1, 2 * GLA_VAL_DIM), lambda bi, p, i: (0, p))]
        args += [prev, gate, gain]
    kernel = functools.partial(_gla_kernel, reverse=reverse, finalize=finalize, chunks=tb // GLA_CHUNK)
    return pl.pallas_call(
        kernel,
        grid=(b, pairs, nb),
        in_specs=in_specs,
        out_specs=[v_spec, st_spec],
        out_shape=[jax.ShapeDtypeStruct((b, l, GLA_V_WIDTH), BF16 if finalize else F32),
                   jax.ShapeDtypeStruct(s0.shape, F32)],
        scratch_shapes=[pltpu.VMEM((2, GLA_VAL_DIM, LANES), F32)],
        compiler_params=_params("parallel", "parallel", "arbitrary"),
        name="gla_scan_bwd" if reverse else "gla_scan_fwd",
    )(*args)


def _na_bias_table(rpb):
    kr, kw = NA_WIN_ROWS, NA_WIN_COLS
    cols = np.arange(GRID_W)
    col_start = np.clip(cols - kw // 2, 0, GRID_W - kw)
    key_cols = np.arange(GRID_W)
    in_window = (key_cols[None, :] >= col_start[:, None]) & (key_cols[None, :] < col_start[:, None] + kw)
    col_off = np.clip(key_cols[None, :] - cols[:, None] + (kw - 1), 0, 2 * kw - 2)
    row_off = np.arange(kr)[None, :] - np.arange(kr)[:, None] + (kr - 1)
    table = rpb[:, row_off[:, :, None, None], col_off[None, None, :, :]]
    table = jnp.where(in_window[None, None, None], table, MASK_VALUE)
    table = table.transpose(0, 1, 3, 2, 4).reshape(NA_HEADS, kr, GRID_W, kr * GRID_W)
    table = table.reshape(NA_HEADS // 2, 2, kr, GRID_W, kr * GRID_W).transpose(0, 2, 1, 3, 4)
    return table.reshape(NA_HEADS // 2, kr, 2 * GRID_W, kr * GRID_W)


def _na_kernel(q_ref, k_ref, v_ref, kc_ref, vc_ref, bias_ref, o_ref, *, rows_per_step, grid_rows):
    kr = NA_WIN_ROWS
    band = kr * GRID_W
    lane = lax.broadcasted_iota(jnp.int32, (GRID_W, LANES), 1)
    first_head = lane < NA_HEAD_DIM
    k_ctx = kc_ref[...]
    v_ctx = vc_ref[...]
    base_row = pl.program_id(2) * rows_per_step

    def one_row(rr, carry):
        r = base_row + rr
        start = jnp.clip(r - kr // 2, 0, grid_rows - kr)
        band_rows = pl.ds(pl.multiple_of(start * GRID_W, GRID_W), band)
        q_rows = pl.ds(pl.multiple_of(rr * GRID_W, GRID_W), GRID_W)
        q = q_ref[q_rows, :]
        zero = jnp.zeros_like(q)
        q2 = jnp.concatenate([jnp.where(first_head, q, zero), jnp.where(first_head, zero, q)], axis=0)
        s_loc = _dot_nt(q2, k_ref[band_rows, :]) + bias_ref[r - start]
        s_ctx = _dot_nt(q2, k_ctx)
        m = jnp.maximum(jnp.max(s_loc, axis=-1, keepdims=True), jnp.max(s_ctx, axis=-1, keepdims=True))
        p_loc = jnp.exp(s_loc - m)
        p_ctx = jnp.exp(s_ctx - m)
        denom = jnp.sum(p_loc, axis=-1, keepdims=True) + jnp.sum(p_ctx, axis=-1, keepdims=True)
        pv = _dot(p_loc.astype(BF16), v_ref[band_rows, :]) + _dot(p_ctx.astype(BF16), v_ctx)
        pv = pv * (1.0 / denom)
        o_ref[q_rows, :] = jnp.where(first_head, pv[:GRID_W], pv[GRID_W:]).astype(o_ref.dtype)
        return carry

    lax.fori_loop(0, rows_per_step, one_row, 0)


def _na_attention(q, k, v, k_ctx, v_ctx, bias, rows_per_step):
    b, l, _ = q.shape
    n_ctx = k_ctx.shape[1]
    grid_rows = l // GRID_W
    tq = rows_per_step * GRID_W
    pairs = NA_HEADS // 2
    q_spec = pl.BlockSpec((None, tq, LANES), lambda bi, p, i: (bi, i, p))
    seq_spec = pl.BlockSpec((None, l, LANES), lambda bi, p, i: (bi, 0, p))
    ctx_spec = pl.BlockSpec((None, n_ctx, LANES), lambda bi, p, i: (bi, 0, p))
    bias_spec = pl.BlockSpec((None,) + bias.shape[1:], lambda bi, p, i: (p, 0, 0, 0))
    kernel = functools.partial(_na_kernel, rows_per_step=rows_per_step, grid_rows=grid_rows)
    return pl.pallas_call(
        kernel,
        grid=(b, pairs, grid_rows // rows_per_step),
        in_specs=[q_spec, seq_spec, seq_spec, ctx_spec, ctx_spec, bias_spec],
        out_specs=q_spec,
        out_shape=jax.ShapeDtypeStruct(q.shape, BF16),
        compiler_params=_params("parallel", "parallel", "arbitrary"),
        name="na_attention",
    )(q, k, v, k_ctx, v_ctx, bias)


def _out_ffn_kernel(x_ref, na_ref, gla_ref, mod_ref, gains_ref, wo_ref, wgu_ref, wd_ref, o_ref, *, hidden_chunks):
    gate_mix, shift_ffn, scale_ffn, gate_ffn = (mod_ref[i:i + 1, :] for i in (2, 3, 4, 5))
    g_post_mix, g_pre_ffn, g_post_ffn = (gains_ref[i:i + 1, :] for i in (0, 1, 2))
    half = na_ref.shape[-1]
    y = _dot(na_ref[...], wo_ref[:half, :]) + _dot(gla_ref[...], wo_ref[half:, :])
    x1 = x_ref[...] + gate_mix * (_rms(y) * g_post_mix)
    h = ((_rms(x1) * g_pre_ffn) * (1.0 + scale_ffn) + shift_ffn).astype(BF16)
    hidden = wd_ref.shape[0]
    acc = None
    for c0, width in hidden_chunks:
        g = _dot(h, wgu_ref[:, c0:c0 + width])
        u = _dot(h, wgu_ref[:, hidden + c0:hidden + c0 + width])
        a = ((g * _sigmoid(g)) * u).astype(BF16)
        part = _dot(a, wd_ref[c0:c0 + width, :])
        acc = part if acc is None else acc + part
    o_ref[...] = x1 + gate_ffn * (_rms(acc) * g_post_ffn)


def _out_ffn(x, na_out, gla_out, mods, gains, w_out, w_gate_up, w_down, tm):
    b, l, d = x.shape
    hidden = w_down.shape[0]
    chunk = 1024
    hidden_chunks = tuple((c0, min(chunk, hidden - c0)) for c0 in range(0, hidden, chunk))
    tok = lambda width: pl.BlockSpec((None, tm, width), lambda bi, i: (bi, i, 0))
    kernel = functools.partial(_out_ffn_kernel, hidden_chunks=hidden_chunks)
    return pl.pallas_call(
        kernel,
        grid=(b, l // tm),
        in_specs=[tok(d), tok(na_out.shape[-1]), tok(gla_out.shape[-1]),
                  pl.BlockSpec((None,) + mods.shape[1:], lambda bi, i: (bi, 0, 0)),
                  _resident(gains.shape), _resident(w_out.shape), _resident(w_gate_up.shape),
                  _resident(w_down.shape)],
        out_specs=tok(d),
        out_shape=jax.ShapeDtypeStruct(x.shape, x.dtype),
        compiler_params=_params("parallel", "parallel"),
        name="out_ffn",
    )(x, na_out, gla_out, mods, gains, w_out, w_gate_up, w_down)


def kernel(x, c, ctx, c_ctx, w_mod, b_mod, norm_pre_mix, norm_post_mix, norm_pre_ffn, norm_post_ffn, w_in, na_rpb, gla_wa2_f, gla_ba_f, gla_wa2_b, gla_ba_b, gla_norm, w_out, w_gate_up, w_down):
    assert w_mod.shape[0] == 1, "single-layer stack: the context stream is only read, never updated"
    b, l, d = x.shape
    n_ctx = ctx.shape[1]
    rank = GLA_GATE_RANK

    cond = jnp.concatenate([c, c_ctx[None, :]], axis=0)
    cond = jnp.pad(cond, ((0, -cond.shape[0] % 8), (0, 0)))
    mods = _adaln_mod(cond, w_mod[0], b_mod[0][None, :])
    lat_mods = mods[:b].reshape(b, 6, d)
    ctx_mods = mods[b:b + 1].reshape(1, 6, d)

    w_in0 = w_in[0]
    w_main = w_in0[:, :MAIN_WIDTH].astype(BF16)
    w_gate = jnp.pad(w_in0[:, MAIN_WIDTH:], ((0, 0), (0, LANES - 2 * rank))).astype(BF16)
    w_a2 = jnp.zeros((LANES, 2 * GLA_QK_WIDTH), F32)
    w_a2 = w_a2.at[:rank, :GLA_QK_WIDTH].set(gla_wa2_f[0]).at[rank:2 * rank, GLA_QK_WIDTH:].set(gla_wa2_b[0])
    w_a2 = w_a2.astype(BF16)
    b_a2 = jnp.concatenate([gla_ba_f[0], gla_ba_b[0]])[None, :]

    proj = functools.partial(_in_proj, gain=norm_pre_mix, w_main=w_main, w_gate=w_gate, w_a2=w_a2, b_a2=b_a2)
    naq, nak, nav, gq, gk, gv, gr, lf, lb = proj(x, lat_mods, rotate=True, tm=min(512, l))
    _, cnk, cnv, _, cgk, cgv, _, clf, clb = proj(ctx, ctx_mods, rotate=False, tm=min(512, n_ctx))

    zero_state = jnp.zeros((b, GLA_HEADS, GLA_VAL_DIM, LANES), F32)
    tb_ctx, tb = min(512, n_ctx), min(512, l)
    _, s_fwd = _gla_scan(cgk, cgk, clf, cgv, zero_state, reverse=False, tb=tb_ctx)
    _, s_bwd = _gla_scan(cgk, cgk, clb, cgv, zero_state, reverse=True, tb=tb_ctx)
    o_fwd, _ = _gla_scan(gq, gk, lf, gv, s_fwd, reverse=False, tb=tb)
    gla_out, _ = _gla_scan(gq, gk, lb, gv, s_bwd, reverse=True, tb=tb, prev=o_fwd, gate=gr, gain=gla_norm)

    na_out = _na_attention(naq, nak, nav, cnk, cnv, _na_bias_table(na_rpb[0]), rows_per_step=min(8, l // GRID_W))

    gains = jnp.concatenate([norm_post_mix, norm_pre_ffn, norm_post_ffn], axis=0)
    return _out_ffn(x, na_out, gla_out, lat_mods, gains, w_out[0].astype(BF16), w_gate_up[0].astype(BF16),
                    w_down[0].astype(BF16), tm=min(512, l))
```

```python
import functools

import numpy as np
import jax
import jax.numpy as jnp
from jax import lax
from jax.experimental import pallas as pl
from jax.experimental.pallas import tpu as pltpu

F32 = jnp.float32
BF16 = jnp.bfloat16

GRID_W = 64
NA_HEADS = 8
NA_HEAD_DIM = 64
NA_WIN_ROWS = 8
NA_WIN_COLS = 16
GLA_HEADS = 4
GLA_KEY_DIM = 64
GLA_VAL_DIM = 128
GLA_GATE_RANK = 16
GLA_GATE_TAU = 16.0
GLA_CHUNK = 64
ROPE_BASE = 10000.0
NORM_EPS = 1e-6

NA_WIDTH = NA_HEADS * NA_HEAD_DIM
GLA_QK_WIDTH = GLA_HEADS * GLA_KEY_DIM
GLA_V_WIDTH = GLA_HEADS * GLA_VAL_DIM
MAIN_WIDTH = 3 * NA_WIDTH + 2 * GLA_QK_WIDTH + 2 * GLA_V_WIDTH

LANES = 128
MASK_VALUE = -1e30
VMEM_LIMIT_BYTES = 56 * 1024 * 1024

NT_DIMS = (((1,), (1,)), ((), ()))
TN_DIMS = (((0,), (0,)), ((), ()))


def _dot(a, b):
    return jnp.dot(a, b, preferred_element_type=F32)


def _dot_nt(a, b):
    return lax.dot_general(a, b, NT_DIMS, preferred_element_type=F32)


def _dot_tn(a, b):
    return lax.dot_general(a, b, TN_DIMS, preferred_element_type=F32)


def _split_bf16(a):
    hi = a.astype(BF16)
    lo = (a - hi.astype(F32)).astype(BF16)
    return hi, lo


def _sigmoid(x):
    return 1.0 / (1.0 + jnp.exp(-x))


def _rms(x):
    return x * lax.rsqrt(jnp.mean(x * x, axis=-1, keepdims=True) + NORM_EPS)


def _params(*semantics):
    return pltpu.CompilerParams(dimension_semantics=semantics, vmem_limit_bytes=VMEM_LIMIT_BYTES)


def _resident(shape):
    zeros = (0,) * len(shape)
    return pl.BlockSpec(shape, lambda *_: zeros)


def _mod_kernel(c_ref, w_ref, b_ref, o_ref):
    c = c_ref[...]
    s_hi, s_lo = _split_bf16(c * _sigmoid(c))
    w_hi, w_lo = _split_bf16(w_ref[...])
    o_ref[...] = _dot(s_hi, w_hi) + (_dot(s_hi, w_lo) + _dot(s_lo, w_hi)) + b_ref[...]


def _adaln_mod(cond, w_mod, b_mod):
    r, d = cond.shape
    n = w_mod.shape[1]
    tn = 1024
    return pl.pallas_call(
        _mod_kernel,
        grid=(n // tn,),
        in_specs=[pl.BlockSpec((r, d), lambda j: (0, 0)),
                  pl.BlockSpec((d, tn), lambda j: (0, j)),
                  pl.BlockSpec((1, tn), lambda j: (0, j))],
        out_specs=pl.BlockSpec((r, tn), lambda j: (0, j)),
        out_shape=jax.ShapeDtypeStruct((r, n), F32),
        compiler_params=_params("arbitrary"),
        name="adaln_mod",
    )(cond, w_mod, b_mod)


def _rope_tables(seq_len, rotate):
    if not rotate:
        return jnp.ones((seq_len, LANES), F32), jnp.zeros((seq_len, LANES), F32)
    quarter = GLA_KEY_DIM // 4
    t = jnp.arange(seq_len)
    inv = ROPE_BASE ** (-jnp.arange(quarter, dtype=F32) / quarter)
    ang_row = (t // GRID_W).astype(F32)[:, None] * inv[None, :]
    ang_col = (t % GRID_W).astype(F32)[:, None] * inv[None, :]
    cos = jnp.concatenate([jnp.cos(ang_row)] * 2 + [jnp.cos(ang_col)] * 2, axis=-1)
    sin = jnp.concatenate([-jnp.sin(ang_row), jnp.sin(ang_row), -jnp.sin(ang_col), jnp.sin(ang_col)], axis=-1)
    reps = LANES // GLA_KEY_DIM
    return jnp.tile(cos, (1, reps)), jnp.tile(sin, (1, reps))


def _in_proj_kernel(x_ref, mod_ref, g_ref, w_ref, wg_ref, wa_ref, ba_ref, cos_ref, sin_ref,
                    naq_ref, nak_ref, nav_ref, gq_ref, gk_ref, gv_ref, gr_ref, lf_ref, lb_ref):
    x = x_ref[...]
    shift, scale = mod_ref[0:1, :], mod_ref[1:2, :]
    h = (_rms(x) * g_ref[...]) * (1.0 + scale) + shift
    hb = h.astype(BF16)

    def proj(c0, width):
        return _dot(hb, w_ref[:, c0:c0 + width])

    naq_ref[...] = (proj(0, NA_WIDTH) * NA_HEAD_DIM ** -0.5).astype(BF16)
    nak_ref[...] = proj(NA_WIDTH, NA_WIDTH).astype(BF16)
    nav_ref[...] = proj(2 * NA_WIDTH, NA_WIDTH).astype(BF16)

    cos, sin = cos_ref[...], sin_ref[...]
    quarter = GLA_KEY_DIM // 4
    lane = lax.broadcasted_iota(jnp.int32, cos.shape, 1)
    first_half = (lane & quarter) == 0

    def rope(t):
        partner = jnp.where(first_half, pltpu.roll(t, LANES - quarter, 1), pltpu.roll(t, quarter, 1))
        return t * cos + partner * sin

    c_gq = 3 * NA_WIDTH
    c_gk = c_gq + GLA_QK_WIDTH
    for j in range(GLA_QK_WIDTH // LANES):
        lanes = slice(j * LANES, (j + 1) * LANES)
        gq_ref[:, lanes] = (rope(proj(c_gq + j * LANES, LANES)) * GLA_KEY_DIM ** -0.5).astype(BF16)
        gk_ref[:, lanes] = rope(proj(c_gk + j * LANES, LANES)).astype(BF16)

    c_gv = c_gk + GLA_QK_WIDTH
    gv_ref[...] = proj(c_gv, GLA_V_WIDTH).astype(BF16)
    gr_ref[...] = proj(c_gv + GLA_V_WIDTH, GLA_V_WIDTH).astype(BF16)

    low_rank = _dot(hb, wg_ref[...]).astype(BF16)
    z = _dot(low_rank, wa_ref[...]) + ba_ref[...]
    log_gate = (jnp.minimum(z, 0.0) - jnp.log(1.0 + jnp.exp(-jnp.abs(z)))) * (1.0 / GLA_GATE_TAU)
    lf_ref[...] = log_gate[:, :GLA_QK_WIDTH]
    lb_ref[...] = log_gate[:, GLA_QK_WIDTH:]


def _in_proj(x, mods, gain, w_main, w_gate, w_a2, b_a2, rotate, tm):
    b, l, d = x.shape
    per_batch = mods.shape[0] > 1
    cos, sin = _rope_tables(l, rotate)
    tok = lambda width: pl.BlockSpec((None, tm, width), lambda bi, i: (bi, i, 0))
    out_widths = [NA_WIDTH] * 3 + [GLA_QK_WIDTH] * 2 + [GLA_V_WIDTH] * 2 + [GLA_QK_WIDTH] * 2
    out_dtypes = [BF16] * 7 + [F32] * 2
    return pl.pallas_call(
        _in_proj_kernel,
        grid=(b, l // tm),
        in_specs=[tok(d),
                  pl.BlockSpec((None,) + mods.shape[1:], lambda bi, i: (bi if per_batch else 0, 0, 0)),
                  _resident(gain.shape), _resident(w_main.shape), _resident(w_gate.shape),
                  _resident(w_a2.shape), _resident(b_a2.shape),
                  pl.BlockSpec((tm, LANES), lambda bi, i: (i, 0)),
                  pl.BlockSpec((tm, LANES), lambda bi, i: (i, 0))],
        out_specs=[tok(w) for w in out_widths],
        out_shape=[jax.ShapeDtypeStruct((b, l, w), dt) for w, dt in zip(out_widths, out_dtypes)],
        compiler_params=_params("parallel", "parallel"),
        name="in_proj",
    )(x, mods, gain, w_main, w_gate, w_a2, b_a2, cos, sin)


def _gla_kernel(*refs, reverse, finalize, chunks):
    if finalize:
        q_ref, k_ref, lg_ref, v_ref, s0_ref, prev_ref, gr_ref, gain_ref, out_ref, sfin_ref, st_ref = refs
    else:
        q_ref, k_ref, lg_ref, v_ref, s0_ref, out_ref, sfin_ref, st_ref = refs
    step = pl.program_id(2)

    @pl.when(step == 0)
    def _():
        st_ref[...] = s0_ref[...]

    c = GLA_CHUNK
    row = lax.broadcasted_iota(jnp.int32, (c, c), 0)
    col = lax.broadcasted_iota(jnp.int32, (c, c), 1)
    tri = (row <= col) if reverse else (row >= col)
    tri_b = tri.astype(BF16)
    lane = lax.broadcasted_iota(jnp.int32, (c, LANES), 1)
    head_lanes = [lane < GLA_KEY_DIM, lane >= GLA_KEY_DIM]
    end_row = 0 if reverse else c - 1

    heads = range(2)
    rows = [slice(i * c, (i + 1) * c) for i in range(chunks)]
    vals = [slice(h * GLA_VAL_DIM, (h + 1) * GLA_VAL_DIM) for h in heads]
    order = list(reversed(range(chunks))) if reverse else list(range(chunks))

    lg_hi, lg_lo = _split_bf16(lg_ref[...])
    side_by_side = lambda a: jnp.concatenate([a[r, :] for r in rows], axis=1)
    bcum_all = _dot(tri_b, side_by_side(lg_hi)) + _dot(tri_b, side_by_side(lg_lo))
    bcum = [bcum_all[:, i * LANES:(i + 1) * LANES] for i in range(chunks)]

    q_heads, k_inv, k_end, decay = [], [], [], []
    for i in range(chunks):
        b_end = bcum[i][end_row:end_row + 1, :]
        q_dec = q_ref[rows[i], :].astype(F32) * jnp.exp(bcum[i])
        k = k_ref[rows[i], :].astype(F32)
        q_heads.append([jnp.where(head_lanes[h], q_dec, 0.0).astype(BF16) for h in heads])
        k_inv.append((k * jnp.exp(-bcum[i])).astype(BF16))
        k_end.append((k * jnp.exp(b_end - bcum[i])).astype(BF16))
        decay.append(jnp.exp(b_end))

    v = [[v_ref[rows[i], vals[h]] for h in heads] for i in range(chunks)]
    attn = [[jnp.where(tri, _dot_nt(q_heads[i][h], k_inv[i]), 0.0).astype(BF16) for h in heads] for i in range(chunks)]
    kv = [[_dot_tn(v[i][h], k_end[i]) for h in heads] for i in range(chunks)]
    o_intra = [[_dot(attn[i][h], v[i][h]) for h in heads] for i in range(chunks)]

    state_before = [[None, None] for _ in range(chunks)]
    for h in heads:
        st = st_ref[h]
        for i in order:
            state_before[i][h] = st.astype(BF16)
            st = st * decay[i] + kv[i][h]
        st_ref[h] = st

    for i in range(chunks):
        for h in heads:
            o = o_intra[i][h] + _dot_nt(q_heads[i][h], state_before[i][h])
            if finalize:
                o = _rms(o + prev_ref[rows[i], vals[h]])
                gate = gr_ref[rows[i], vals[h]].astype(F32)
                out_ref[rows[i], vals[h]] = (o * gain_ref[:, vals[h]] * (gate * _sigmoid(gate))).astype(out_ref.dtype)
            else:
                out_ref[rows[i], vals[h]] = o

    @pl.when(step == pl.num_programs(2) - 1)
    def _():
        sfin_ref[...] = st_ref[...]


def _gla_scan(q, k, lg, v, s0, *, reverse, tb, prev=None, gate=None, gain=None):
    b, l, _ = q.shape
    finalize = prev is not None
    nb = l // tb
    pairs = GLA_HEADS // 2
    blk = (lambda i: nb - 1 - i) if reverse else (lambda i: i)
    qk_spec = pl.BlockSpec((None, tb, LANES), lambda bi, p, i: (bi, blk(i), p))
    v_spec = pl.BlockSpec((None, tb, 2 * GLA_VAL_DIM), lambda bi, p, i: (bi, blk(i), p))
    st_spec = pl.BlockSpec((None, 2, GLA_VAL_DIM, LANES), lambda bi, p, i: (bi, p, 0, 0))
    in_specs = [qk_spec, qk_spec, qk_spec, v_spec, st_spec]
    args = [q, k, lg, v, s0]
    if finalize:
        in_specs += [v_spec, v_spec, pl.BlockSpec((1, 2 * GLA_VAL_DIM), lambda bi, p, i: (0, p))]
        args += [prev, gate, gain]
    kernel = functools.partial(_gla_kernel, reverse=reverse, finalize=finalize, chunks=tb // GLA_CHUNK)
    return pl.pallas_call(
        kernel,
        grid=(b, pairs, nb),
        in_specs=in_specs,
        out_specs=[v_spec, st_spec],
        out_shape=[jax.ShapeDtypeStruct((b, l, GLA_V_WIDTH), BF16 if finalize else F32),
                   jax.ShapeDtypeStruct(s0.shape, F32)],
        scratch_shapes=[pltpu.VMEM((2, GLA_VAL_DIM, LANES), F32)],
        compiler_params=_params("parallel", "parallel", "arbitrary"),
        name="gla_scan_bwd" if reverse else "gla_scan_fwd",
    )(*args)


def _na_bias_table(rpb):
    kr, kw, w = NA_WIN_ROWS, NA_WIN_COLS, GRID_W
    n_rows = 2 * kr - 1
    cols = np.arange(w)
    col_start = np.clip(cols - kw // 2, 0, w - kw)
    in_window = (cols[None, :] >= col_start[:, None]) & (cols[None, :] < col_start[:, None] + kw)
    lead = w - kw
    padded = jnp.pad(rpb, ((0, 0), (0, 0), (lead, 2 * w - lead - (2 * kw - 1))))
    flat = jnp.broadcast_to(padded[:, :, None, :], (NA_HEADS, n_rows, w, 2 * w)).reshape(NA_HEADS, n_rows, 2 * w * w)
    toe = flat[:, :, w - 1:w - 1 + w * (2 * w - 1)].reshape(NA_HEADS, n_rows, w, 2 * w - 1)[..., :w]
    toe = jnp.where(in_window[None, None], toe, MASK_VALUE).transpose(0, 2, 1, 3)
    table = jnp.stack([toe[:, :, kr - 1 - d:2 * kr - 1 - d, :].reshape(NA_HEADS, w, kr * w) for d in range(kr)], axis=1)
    table = table.reshape(NA_HEADS // 2, 2, kr, w, kr * w).transpose(0, 2, 1, 3, 4)
    return table.reshape(NA_HEADS // 2, kr, 2 * w, kr * w)


def _na_kernel(q_ref, k_ref, v_ref, kc_ref, vc_ref, bias_ref, o_ref, *, rows_per_step, grid_rows):
    kr, w = NA_WIN_ROWS, GRID_W
    band = kr * w
    lane = lax.broadcasted_iota(jnp.int32, (w, LANES), 1)
    first_head = lane < NA_HEAD_DIM
    base_row = pl.program_id(2) * rows_per_step
    steps = range(rows_per_step)

    stacked = []
    for i in steps:
        q = q_ref[i * w:(i + 1) * w, :]
        zero = jnp.zeros_like(q)
        stacked += [jnp.where(first_head, q, zero), jnp.where(first_head, zero, q)]
    q2 = jnp.concatenate(stacked, axis=0)
    pair = [slice(i * 2 * w, (i + 1) * 2 * w) for i in steps]

    s_ctx = _dot_nt(q2, kc_ref[...])
    band_rows, s_loc = [], []
    for i in steps:
        r = base_row + i
        start = jnp.clip(r - kr // 2, 0, grid_rows - kr)
        band_rows.append(pl.ds(pl.multiple_of(start * w, w), band))
        s_loc.append(_dot_nt(q2[pair[i], :], k_ref[band_rows[i], :]) + bias_ref[r - start])

    m = jnp.max(s_ctx, axis=-1, keepdims=True)
    p_loc, p_ctx, denom = [], [], []
    for i in steps:
        m_i = jnp.maximum(jnp.max(s_loc[i], axis=-1, keepdims=True), m[pair[i], :])
        p = jnp.exp(s_loc[i] - m_i)
        p_c = jnp.exp(s_ctx[pair[i], :] - m_i)
        denom.append(jnp.sum(p, axis=-1, keepdims=True) + jnp.sum(p_c, axis=-1, keepdims=True))
        p_loc.append(p.astype(BF16))
        p_ctx.append(p_c.astype(BF16))
    pv_ctx = _dot(jnp.concatenate(p_ctx, axis=0), vc_ref[...])

    for i in steps:
        pv = (_dot(p_loc[i], v_ref[band_rows[i], :]) + pv_ctx[pair[i], :]) * (1.0 / denom[i])
        o_ref[i * w:(i + 1) * w, :] = jnp.where(first_head, pv[:w], pv[w:]).astype(o_ref.dtype)


def _na_attention(q, k, v, k_ctx, v_ctx, bias, rows_per_step):
    b, l, _ = q.shape
    n_ctx = k_ctx.shape[1]
    grid_rows = l // GRID_W
    tq = rows_per_step * GRID_W
    pairs = NA_HEADS // 2
    q_spec = pl.BlockSpec((None, tq, LANES), lambda bi, p, i: (bi, i, p))
    seq_spec = pl.BlockSpec((None, l, LANES), lambda bi, p, i: (bi, 0, p))
    ctx_spec = pl.BlockSpec((None, n_ctx, LANES), lambda bi, p, i: (bi, 0, p))
    bias_spec = pl.BlockSpec((None,) + bias.shape[1:], lambda bi, p, i: (p, 0, 0, 0))
    kernel = functools.partial(_na_kernel, rows_per_step=rows_per_step, grid_rows=grid_rows)
    return pl.pallas_call(
        kernel,
        grid=(b, pairs, grid_rows // rows_per_step),
        in_specs=[q_spec, seq_spec, seq_spec, ctx_spec, ctx_spec, bias_spec],
        out_specs=q_spec,
        out_shape=jax.ShapeDtypeStruct(q.shape, BF16),
        compiler_params=_params("parallel", "parallel", "arbitrary"),
        name="na_attention",
    )(q, k, v, k_ctx, v_ctx, bias)


def _out_ffn_kernel(x_ref, na_ref, gla_ref, mod_ref, gains_ref, wo_ref, wgu_ref, wd_ref, o_ref, *, hidden_chunks):
    gate_mix, shift_ffn, scale_ffn, gate_ffn = (mod_ref[i:i + 1, :] for i in (2, 3, 4, 5))
    g_post_mix, g_pre_ffn, g_post_ffn = (gains_ref[i:i + 1, :] for i in (0, 1, 2))
    half = na_ref.shape[-1]
    y = _dot(na_ref[...], wo_ref[:half, :]) + _dot(gla_ref[...], wo_ref[half:, :])
    x1 = x_ref[...] + gate_mix * (_rms(y) * g_post_mix)
    h = ((_rms(x1) * g_pre_ffn) * (1.0 + scale_ffn) + shift_ffn).astype(BF16)
    hidden = wd_ref.shape[0]
    acc = None
    for c0, width in hidden_chunks:
        g = _dot(h, wgu_ref[:, c0:c0 + width])
        u = _dot(h, wgu_ref[:, hidden + c0:hidden + c0 + width])
        a = ((g * _sigmoid(g)) * u).astype(BF16)
        part = _dot(a, wd_ref[c0:c0 + width, :])
        acc = part if acc is None else acc + part
    o_ref[...] = x1 + gate_ffn * (_rms(acc) * g_post_ffn)


def _out_ffn(x, na_out, gla_out, mods, gains, w_out, w_gate_up, w_down, tm):
    b, l, d = x.shape
    hidden = w_down.shape[0]
    chunk = 1024
    hidden_chunks = tuple((c0, min(chunk, hidden - c0)) for c0 in range(0, hidden, chunk))
    tok = lambda width: pl.BlockSpec((None, tm, width), lambda bi, i: (bi, i, 0))
    kernel = functools.partial(_out_ffn_kernel, hidden_chunks=hidden_chunks)
    return pl.pallas_call(
        kernel,
        grid=(b, l // tm),
        in_specs=[tok(d), tok(na_out.shape[-1]), tok(gla_out.shape[-1]),
                  pl.BlockSpec((None,) + mods.shape[1:], lambda bi, i: (bi, 0, 0)),
                  _resident(gains.shape), _resident(w_out.shape), _resident(w_gate_up.shape),
                  _resident(w_down.shape)],
        out_specs=tok(d),
        out_shape=jax.ShapeDtypeStruct(x.shape, x.dtype),
        compiler_params=_params("parallel", "parallel"),
        name="out_ffn",
    )(x, na_out, gla_out, mods, gains, w_out, w_gate_up, w_down)


def kernel(x, c, ctx, c_ctx, w_mod, b_mod, norm_pre_mix, norm_post_mix, norm_pre_ffn, norm_post_ffn, w_in, na_rpb, gla_wa2_f, gla_ba_f, gla_wa2_b, gla_ba_b, gla_norm, w_out, w_gate_up, w_down):
    assert w_mod.shape[0] == 1, "single-layer stack: the context stream is only read, never updated"
    b, l, d = x.shape
    n_ctx = ctx.shape[1]
    rank = GLA_GATE_RANK

    cond = jnp.concatenate([c, c_ctx[None, :]], axis=0)
    cond = jnp.pad(cond, ((0, -cond.shape[0] % 8), (0, 0)))
    mods = _adaln_mod(cond, w_mod[0], b_mod[0][None, :])
    lat_mods = mods[:b].reshape(b, 6, d)
    ctx_mods = mods[b:b + 1].reshape(1, 6, d)

    w_in0 = w_in[0]
    w_main = w_in0[:, :MAIN_WIDTH].astype(BF16)
    w_gate = jnp.pad(w_in0[:, MAIN_WIDTH:], ((0, 0), (0, LANES - 2 * rank))).astype(BF16)
    w_a2 = jnp.zeros((LANES, 2 * GLA_QK_WIDTH), F32)
    w_a2 = w_a2.at[:rank, :GLA_QK_WIDTH].set(gla_wa2_f[0]).at[rank:2 * rank, GLA_QK_WIDTH:].set(gla_wa2_b[0])
    w_a2 = w_a2.astype(BF16)
    b_a2 = jnp.concatenate([gla_ba_f[0], gla_ba_b[0]])[None, :]

    proj = functools.partial(_in_proj, gain=norm_pre_mix, w_main=w_main, w_gate=w_gate, w_a2=w_a2, b_a2=b_a2)
    naq, nak, nav, gq, gk, gv, gr, lf, lb = proj(x, lat_mods, rotate=True, tm=min(512, l))
    _, cnk, cnv, _, cgk, cgv, _, clf, clb = proj(ctx, ctx_mods, rotate=False, tm=min(512, n_ctx))

    zero_state = jnp.zeros((b, GLA_HEADS, GLA_VAL_DIM, LANES), F32)
    tb_ctx, tb = min(512, n_ctx), min(512, l)
    _, s_fwd = _gla_scan(cgk, cgk, clf, cgv, zero_state, reverse=False, tb=tb_ctx)
    _, s_bwd = _gla_scan(cgk, cgk, clb, cgv, zero_state, reverse=True, tb=tb_ctx)
    o_fwd, _ = _gla_scan(gq, gk, lf, gv, s_fwd, reverse=False, tb=tb)
    gla_out, _ = _gla_scan(gq, gk, lb, gv, s_bwd, reverse=True, tb=tb, prev=o_fwd, gate=gr, gain=gla_norm)

    na_out = _na_attention(naq, nak, nav, cnk, cnv, _na_bias_table(na_rpb[0]), rows_per_step=min(8, l // GRID_W))

    gains = jnp.concatenate([norm_post_mix, norm_pre_ffn, norm_post_ffn], axis=0)
    return _out_ffn(x, na_out, gla_out, lat_mods, gains, w_out[0].astype(BF16), w_gate_up[0].astype(BF16),
                    w_down[0].astype(BF16), tm=min(512, l))
```

```python
import functools

import numpy as np
import jax
import jax.numpy as jnp
from jax import lax
from jax.experimental import pallas as pl
from jax.experimental.pallas import tpu as pltpu

F32 = jnp.float32
BF16 = jnp.bfloat16

GRID_W = 64
NA_HEADS = 8
NA_HEAD_DIM = 64
NA_WIN_ROWS = 8
NA_WIN_COLS = 16
GLA_HEADS = 4
GLA_KEY_DIM = 64
GLA_VAL_DIM = 128
GLA_GATE_RANK = 16
GLA_GATE_TAU = 16.0
GLA_CHUNK = 64
ROPE_BASE = 10000.0
NORM_EPS = 1e-6

NA_WIDTH = NA_HEADS * NA_HEAD_DIM
GLA_QK_WIDTH = GLA_HEADS * GLA_KEY_DIM
GLA_V_WIDTH = GLA_HEADS * GLA_VAL_DIM

LANES = 128
MASK_VALUE = -1e30
LOG2_E = 1.4426950408889634
NA_Q_SCALE = NA_HEAD_DIM ** -0.5 * LOG2_E
VMEM_LIMIT_BYTES = 56 * 1024 * 1024

NT_DIMS = (((1,), (1,)), ((), ()))
TN_DIMS = (((0,), (0,)), ((), ()))


def _dot(a, b):
    return jnp.dot(a, b, preferred_element_type=F32)


def _dot_nt(a, b):
    return lax.dot_general(a, b, NT_DIMS, preferred_element_type=F32)


def _dot_tn(a, b):
    return lax.dot_general(a, b, TN_DIMS, preferred_element_type=F32)


def _split_bf16(a):
    hi = a.astype(BF16)
    lo = (a - hi.astype(F32)).astype(BF16)
    return hi, lo


def _sigmoid(x):
    return 1.0 / (1.0 + jnp.exp(-x))


def _rms(x):
    return x * lax.rsqrt(jnp.mean(x * x, axis=-1, keepdims=True) + NORM_EPS)


def _params(*semantics):
    return pltpu.CompilerParams(dimension_semantics=semantics, vmem_limit_bytes=VMEM_LIMIT_BYTES)


def _resident(shape):
    zeros = (0,) * len(shape)
    return pl.BlockSpec(shape, lambda *_: zeros)


def _mod_kernel(c_ref, w_ref, b_ref, o_ref):
    c = c_ref[...]
    s_hi, s_lo = _split_bf16(c * _sigmoid(c))
    w_hi, w_lo = _split_bf16(w_ref[...])
    o_ref[...] = _dot(s_hi, w_hi) + (_dot(s_hi, w_lo) + _dot(s_lo, w_hi)) + b_ref[...]


def _adaln_mod(cond, w_mod, b_mod):
    r, d = cond.shape
    n = w_mod.shape[1]
    tn = 1024
    return pl.pallas_call(
        _mod_kernel,
        grid=(n // tn,),
        in_specs=[pl.BlockSpec((r, d), lambda j: (0, 0)),
                  pl.BlockSpec((d, tn), lambda j: (0, j)),
                  pl.BlockSpec((1, tn), lambda j: (0, j))],
        out_specs=pl.BlockSpec((r, tn), lambda j: (0, j)),
        out_shape=jax.ShapeDtypeStruct((r, n), F32),
        compiler_params=_params("arbitrary"),
        name="adaln_mod",
    )(cond, w_mod, b_mod)


def _rope_tables(seq_len, rotate):
    if not rotate:
        return jnp.ones((seq_len, LANES), F32), jnp.zeros((seq_len, LANES), F32)
    quarter = GLA_KEY_DIM // 4
    t = jnp.arange(seq_len)
    inv = ROPE_BASE ** (-jnp.arange(quarter, dtype=F32) / quarter)
    ang_row = (t // GRID_W).astype(F32)[:, None] * inv[None, :]
    ang_col = (t % GRID_W).astype(F32)[:, None] * inv[None, :]
    cos = jnp.concatenate([jnp.cos(ang_row)] * 2 + [jnp.cos(ang_col)] * 2, axis=-1)
    sin = jnp.concatenate([-jnp.sin(ang_row), jnp.sin(ang_row), -jnp.sin(ang_col), jnp.sin(ang_col)], axis=-1)
    reps = LANES // GLA_KEY_DIM
    return jnp.tile(cos, (1, reps)), jnp.tile(sin, (1, reps))


def _in_proj_kernel(x_ref, mod_ref, g_ref, w_ref, wa_ref, ba_ref, cos_ref, sin_ref,
                    naq_ref, nak_ref, nav_ref, gq_ref, gk_ref, gv_ref, gr_ref, lf_ref, lb_ref):
    x = x_ref[...]
    shift, scale = mod_ref[0:1, :], mod_ref[1:2, :]
    h = (_rms(x) * g_ref[...]) * (1.0 + scale) + shift
    hb = h.astype(BF16)

    def proj(c0, width):
        return _dot(hb, w_ref[:, c0:c0 + width])

    naq_ref[...] = (proj(0, NA_WIDTH) * NA_Q_SCALE).astype(BF16)
    nak_ref[...] = proj(NA_WIDTH, NA_WIDTH).astype(BF16)
    nav_ref[...] = proj(2 * NA_WIDTH, NA_WIDTH).astype(BF16)

    cos, sin = cos_ref[...], sin_ref[...]
    quarter = GLA_KEY_DIM // 4
    lane = lax.broadcasted_iota(jnp.int32, cos.shape, 1)
    first_half = (lane & quarter) == 0

    def rope(t):
        partner = jnp.where(first_half, pltpu.roll(t, LANES - quarter, 1), pltpu.roll(t, quarter, 1))
        return t * cos + partner * sin

    c_gq = 3 * NA_WIDTH
    gqk = proj(c_gq, 2 * GLA_QK_WIDTH)
    for j in range(GLA_QK_WIDTH // LANES):
        lanes = slice(j * LANES, (j + 1) * LANES)
        k_lanes = slice(GLA_QK_WIDTH + j * LANES, GLA_QK_WIDTH + (j + 1) * LANES)
        gq_ref[:, lanes] = (rope(gqk[:, lanes]) * GLA_KEY_DIM ** -0.5).astype(BF16)
        gk_ref[:, lanes] = rope(gqk[:, k_lanes]).astype(BF16)

    c_gv = c_gq + 2 * GLA_QK_WIDTH
    gv_ref[...] = proj(c_gv, GLA_V_WIDTH).astype(BF16)
    gr_gate = proj(c_gv + GLA_V_WIDTH, GLA_V_WIDTH + LANES)
    gr_ref[...] = gr_gate[:, :GLA_V_WIDTH].astype(BF16)

    low_rank = gr_gate[:, GLA_V_WIDTH:].astype(BF16)
    z = _dot(low_rank, wa_ref[...]) + ba_ref[...]
    log_gate = (jnp.minimum(z, 0.0) - jnp.log(1.0 + jnp.exp(-jnp.abs(z)))) * (1.0 / GLA_GATE_TAU)
    lf_ref[...] = log_gate[:, :GLA_QK_WIDTH]
    lb_ref[...] = log_gate[:, GLA_QK_WIDTH:]


def _in_proj(x, mods, gain, w_all, w_a2, b_a2, rotate, tm):
    b, l, d = x.shape
    per_batch = mods.shape[0] > 1
    cos, sin = _rope_tables(l, rotate)
    tok = lambda width: pl.BlockSpec((None, tm, width), lambda bi, i: (bi, i, 0))
    out_widths = [NA_WIDTH] * 3 + [GLA_QK_WIDTH] * 2 + [GLA_V_WIDTH] * 2 + [GLA_QK_WIDTH] * 2
    out_dtypes = [BF16] * 7 + [F32] * 2
    return pl.pallas_call(
        _in_proj_kernel,
        grid=(b, l // tm),
        in_specs=[tok(d),
                  pl.BlockSpec((None,) + mods.shape[1:], lambda bi, i: (bi if per_batch else 0, 0, 0)),
                  _resident(gain.shape), _resident(w_all.shape),
                  _resident(w_a2.shape), _resident(b_a2.shape),
                  pl.BlockSpec((tm, LANES), lambda bi, i: (i, 0)),
                  pl.BlockSpec((tm, LANES), lambda bi, i: (i, 0))],
        out_specs=[tok(w) for w in out_widths],
        out_shape=[jax.ShapeDtypeStruct((b, l, w), dt) for w, dt in zip(out_widths, out_dtypes)],
        compiler_params=_params("parallel", "parallel"),
        name="in_proj",
    )(x, mods, gain, w_all, w_a2, b_a2, cos, sin)


def _gla_kernel(*refs, reverse, finalize, chunks):
    if finalize:
        q_ref, k_ref, lg_ref, v_ref, s0_ref, prev_ref, gr_ref, gain_ref, out_ref, sfin_ref, st_ref = refs
    else:
        q_ref, k_ref, lg_ref, v_ref, s0_ref, out_ref, sfin_ref, st_ref = refs
    step = pl.program_id(2)

    @pl.when(step == 0)
    def _():
        st_ref[...] = s0_ref[...]

    c = GLA_CHUNK
    row = lax.broadcasted_iota(jnp.int32, (c, c), 0)
    col = lax.broadcasted_iota(jnp.int32, (c, c), 1)
    tri = (row <= col) if reverse else (row >= col)
    tri_b = tri.astype(BF16)
    lane = lax.broadcasted_iota(jnp.int32, (c, LANES), 1)
    head_lanes = [lane < GLA_KEY_DIM, lane >= GLA_KEY_DIM]
    end_row = 0 if reverse else c - 1

    heads = range(2)
    rows = [slice(i * c, (i + 1) * c) for i in range(chunks)]
    vals = [slice(h * GLA_VAL_DIM, (h + 1) * GLA_VAL_DIM) for h in heads]
    order = list(reversed(range(chunks))) if reverse else list(range(chunks))

    lg_hi, lg_lo = _split_bf16(lg_ref[...])
    side_by_side = lambda a: jnp.concatenate([a[r, :] for r in rows], axis=1)
    bcum_all = _dot(tri_b, side_by_side(lg_hi)) + _dot(tri_b, side_by_side(lg_lo))
    bcum = [bcum_all[:, i * LANES:(i + 1) * LANES] for i in range(chunks)]

    q_heads, k_inv, k_end, decay = [], [], [], []
    for i in range(chunks):
        b_end = bcum[i][end_row:end_row + 1, :]
        q_dec = q_ref[rows[i], :].astype(F32) * jnp.exp(bcum[i])
        k = k_ref[rows[i], :].astype(F32)
        q_heads.append([jnp.where(head_lanes[h], q_dec, 0.0).astype(BF16) for h in heads])
        k_inv.append((k * jnp.exp(-bcum[i])).astype(BF16))
        k_end.append((k * jnp.exp(b_end - bcum[i])).astype(BF16))
        decay.append(jnp.exp(b_end))

    v_t = [[v_ref[rows[i], vals[h]].T for h in heads] for i in range(chunks)]
    attn = [[jnp.where(tri, _dot_nt(q_heads[i][h], k_inv[i]), 0.0).astype(BF16) for h in heads] for i in range(chunks)]
    kv = [[_dot(v_t[i][h], k_end[i]) for h in heads] for i in range(chunks)]

    state_before = [[None, None] for _ in range(chunks)]
    for h in heads:
        st = st_ref[h]
        for i in order:
            state_before[i][h] = st.astype(BF16)
            st = st * decay[i] + kv[i][h]
        st_ref[h] = st

    for i in range(chunks):
        for h in heads:
            lhs = jnp.concatenate([q_heads[i][h], attn[i][h]], axis=1)
            rhs = jnp.concatenate([state_before[i][h], v_t[i][h]], axis=1)
            o = _dot_nt(lhs, rhs)
            if finalize:
                o = _rms(o + prev_ref[rows[i], vals[h]])
                gate = gr_ref[rows[i], vals[h]].astype(F32)
                out_ref[rows[i], vals[h]] = (o * gain_ref[:, vals[h]] * (gate * _sigmoid(gate))).astype(out_ref.dtype)
            else:
                out_ref[rows[i], vals[h]] = o

    @pl.when(step == pl.num_programs(2) - 1)
    def _():
        sfin_ref[...] = st_ref[...]


def _gla_scan(q, k, lg, v, s0, *, reverse, tb, prev=None, gate=None, gain=None):
    b, l, _ = q.shape
    finalize = prev is not None
    nb = l // tb
    pairs = GLA_HEADS // 2
    blk = (lambda i: nb - 1 - i) if reverse else (lambda i: i)
    qk_spec = pl.BlockSpec((None, tb, LANES), lambda bi, p, i: (bi, blk(i), p))
    v_spec = pl.BlockSpec((None, tb, 2 * GLA_VAL_DIM), lambda bi, p, i: (bi, blk(i), p))
    st_spec = pl.BlockSpec((None, 2, GLA_VAL_DIM, LANES), lambda bi, p, i: (bi, p, 0, 0))
    in_specs = [qk_spec, qk_spec, qk_spec, v_spec, st_spec]
    args = [q, k, lg, v, s0]
    if finalize:
        in_specs += [v_spec, v_spec, pl.BlockSpec((1, 2 * GLA_VAL_DIM), lambda bi, p, i: (0, p))]
        args += [prev, gate, gain]
    kernel = functools.partial(_gla_kernel, reverse=reverse, finalize=finalize, chunks=tb // GLA_CHUNK)
    return pl.pallas_call(
        kernel,
        grid=(b, pairs, nb),
        in_specs=in_specs,
        out_specs=[v_spec, st_spec],
        out_shape=[jax.ShapeDtypeStruct((b, l, GLA_V_WIDTH), BF16 if finalize else F32),
                   jax.ShapeDtypeStruct(s0.shape, F32)],
        scratch_shapes=[pltpu.VMEM((2, GLA_VAL_DIM, LANES), F32)],
        compiler_params=_params("parallel", "parallel", "arbitrary"),
        name="gla_scan_bwd" if reverse else "gla_scan_fwd",
    )(*args)


def _na_bias_table(rpb):
    kr, kw, w = NA_WIN_ROWS, NA_WIN_COLS, GRID_W
    n_rows = 2 * kr - 1
    cols = np.arange(w)
    col_start = np.clip(cols - kw // 2, 0, w - kw)
    in_window = (cols[None, :] >= col_start[:, None]) & (cols[None, :] < col_start[:, None] + kw)
    lead = w - kw
    padded = jnp.pad(rpb, ((0, 0), (0, 0), (lead, 2 * w - lead - (2 * kw - 1))))
    flat = jnp.broadcast_to(padded[:, :, None, :], (NA_HEADS, n_rows, w, 2 * w)).reshape(NA_HEADS, n_rows, 2 * w * w)
    toe = flat[:, :, w - 1:w - 1 + w * (2 * w - 1)].reshape(NA_HEADS, n_rows, w, 2 * w - 1)[..., :w]
    toe = jnp.where(in_window[None, None], toe * LOG2_E, MASK_VALUE).transpose(0, 2, 1, 3)
    return jnp.stack([toe[:, :, kr - 1 - d:2 * kr - 1 - d, :].reshape(NA_HEADS, w, kr * w) for d in range(kr)], axis=1)


def _na_kernel(q_ref, k_ref, v_ref, kc_ref, vc_ref, bias_ref, o_ref, *, rows_per_step, grid_rows):
    kr, w = NA_WIN_ROWS, GRID_W
    band = kr * w
    lane = lax.broadcasted_iota(jnp.int32, (w, LANES), 1)
    first_head = lane < NA_HEAD_DIM
    base_row = pl.program_id(2) * rows_per_step
    steps = range(rows_per_step)

    stacked = []
    for i in steps:
        q = q_ref[i * w:(i + 1) * w, :]
        zero = jnp.zeros_like(q)
        stacked += [jnp.where(first_head, q, zero), jnp.where(first_head, zero, q)]
    q2 = jnp.concatenate(stacked, axis=0)
    pair = [slice(i * 2 * w, (i + 1) * 2 * w) for i in steps]

    s_ctx = _dot_nt(q2, kc_ref[...])
    band_rows, s_loc = [], []
    for i in steps:
        r = base_row + i
        start = jnp.clip(r - kr // 2, 0, grid_rows - kr)
        band_rows.append(pl.ds(pl.multiple_of(start * w, w), band))
        bias = jnp.concatenate([bias_ref[0, r - start], bias_ref[1, r - start]], axis=0)
        s_loc.append(_dot_nt(q2[pair[i], :], k_ref[band_rows[i], :]) + bias)

    m = jnp.max(s_ctx, axis=-1, keepdims=True)
    p_loc, p_ctx, denom = [], [], []
    for i in steps:
        m_i = jnp.maximum(jnp.max(s_loc[i], axis=-1, keepdims=True), m[pair[i], :])
        p = jnp.exp2(s_loc[i] - m_i)
        p_c = jnp.exp2(s_ctx[pair[i], :] - m_i)
        denom.append(jnp.sum(p, axis=-1, keepdims=True) + jnp.sum(p_c, axis=-1, keepdims=True))
        p_loc.append(p.astype(BF16))
        p_ctx.append(p_c.astype(BF16))
    pv_ctx = _dot(jnp.concatenate(p_ctx, axis=0), vc_ref[...])

    for i in steps:
        pv = (_dot(p_loc[i], v_ref[band_rows[i], :]) + pv_ctx[pair[i], :]) * (1.0 / denom[i])
        o_ref[i * w:(i + 1) * w, :] = jnp.where(first_head, pv[:w], pv[w:]).astype(o_ref.dtype)


def _na_attention(q, k, v, k_ctx, v_ctx, bias, rows_per_step):
    b, l, _ = q.shape
    n_ctx = k_ctx.shape[1]
    grid_rows = l // GRID_W
    tq = rows_per_step * GRID_W
    pairs = NA_HEADS // 2
    q_spec = pl.BlockSpec((None, tq, LANES), lambda bi, p, i: (bi, i, p))
    seq_spec = pl.BlockSpec((None, l, LANES), lambda bi, p, i: (bi, 0, p))
    ctx_spec = pl.BlockSpec((None, n_ctx, LANES), lambda bi, p, i: (bi, 0, p))
    bias_spec = pl.BlockSpec((2,) + bias.shape[1:], lambda bi, p, i: (p, 0, 0, 0))
    kernel = functools.partial(_na_kernel, rows_per_step=rows_per_step, grid_rows=grid_rows)
    return pl.pallas_call(
        kernel,
        grid=(b, pairs, grid_rows // rows_per_step),
        in_specs=[q_spec, seq_spec, seq_spec, ctx_spec, ctx_spec, bias_spec],
        out_specs=q_spec,
        out_shape=jax.ShapeDtypeStruct(q.shape, BF16),
        compiler_params=_params("parallel", "parallel", "arbitrary"),
        name="na_attention",
    )(q, k, v, k_ctx, v_ctx, bias)


def _out_ffn_kernel(x_ref, na_ref, gla_ref, mod_ref, gains_ref, wo_ref, wgu_ref, wd_ref, o_ref, *, hidden_chunks):
    gate_mix, shift_ffn, scale_ffn, gate_ffn = (mod_ref[i:i + 1, :] for i in (2, 3, 4, 5))
    g_post_mix, g_pre_ffn, g_post_ffn = (gains_ref[i:i + 1, :] for i in (0, 1, 2))
    half = na_ref.shape[-1]
    y = _dot(na_ref[...], wo_ref[:half, :]) + _dot(gla_ref[...], wo_ref[half:, :])
    x1 = x_ref[...] + gate_mix * (_rms(y) * g_post_mix)
    h = ((_rms(x1) * g_pre_ffn) * (1.0 + scale_ffn) + shift_ffn).astype(BF16)
    hidden = wd_ref.shape[0]
    acc = None
    for c0, width in hidden_chunks:
        g = _dot(h, wgu_ref[:, c0:c0 + width])
        u = _dot(h, wgu_ref[:, hidden + c0:hidden + c0 + width])
        a = ((g * _sigmoid(g)) * u).astype(BF16)
        part = _dot(a, wd_ref[c0:c0 + width, :])
        acc = part if acc is None else acc + part
    o_ref[...] = x1 + gate_ffn * (_rms(acc) * g_post_ffn)


def _out_ffn(x, na_out, gla_out, mods, gains, w_out, w_gate_up, w_down, tm):
    b, l, d = x.shape
    hidden = w_down.shape[0]
    chunk = 1024
    hidden_chunks = tuple((c0, min(chunk, hidden - c0)) for c0 in range(0, hidden, chunk))
    tok = lambda width: pl.BlockSpec((None, tm, width), lambda bi, i: (bi, i, 0))
    kernel = functools.partial(_out_ffn_kernel, hidden_chunks=hidden_chunks)
    return pl.pallas_call(
        kernel,
        grid=(b, l // tm),
        in_specs=[tok(d), tok(na_out.shape[-1]), tok(gla_out.shape[-1]),
                  pl.BlockSpec((None,) + mods.shape[1:], lambda bi, i: (bi, 0, 0)),
                  _resident(gains.shape), _resident(w_out.shape), _resident(w_gate_up.shape),
                  _resident(w_down.shape)],
        out_specs=tok(d),
        out_shape=jax.ShapeDtypeStruct(x.shape, x.dtype),
        compiler_params=_params("parallel", "parallel"),
        name="out_ffn",
    )(x, na_out, gla_out, mods, gains, w_out, w_gate_up, w_down)


def kernel(x, c, ctx, c_ctx, w_mod, b_mod, norm_pre_mix, norm_post_mix, norm_pre_ffn, norm_post_ffn, w_in, na_rpb, gla_wa2_f, gla_ba_f, gla_wa2_b, gla_ba_b, gla_norm, w_out, w_gate_up, w_down):
    assert w_mod.shape[0] == 1, "single-layer stack: the context stream is only read, never updated"
    b, l, d = x.shape
    n_ctx = ctx.shape[1]
    rank = GLA_GATE_RANK

    cond = jnp.concatenate([c, c_ctx[None, :]], axis=0)
    cond = jnp.pad(cond, ((0, -cond.shape[0] % 8), (0, 0)))
    mods = _adaln_mod(cond, w_mod[0], b_mod[0][None, :])
    lat_mods = mods[:b].reshape(b, 6, d)
    ctx_mods = mods[b:b + 1].reshape(1, 6, d)

    w_all = jnp.pad(w_in[0], ((0, 0), (0, LANES - 2 * rank))).astype(BF16)
    w_a2 = jnp.zeros((LANES, 2 * GLA_QK_WIDTH), F32)
    w_a2 = w_a2.at[:rank, :GLA_QK_WIDTH].set(gla_wa2_f[0]).at[rank:2 * rank, GLA_QK_WIDTH:].set(gla_wa2_b[0])
    w_a2 = w_a2.astype(BF16)
    b_a2 = jnp.concatenate([gla_ba_f[0], gla_ba_b[0]])[None, :]

    proj = functools.partial(_in_proj, gain=norm_pre_mix, w_all=w_all, w_a2=w_a2, b_a2=b_a2)
    naq, nak, nav, gq, gk, gv, gr, lf, lb = proj(x, lat_mods, rotate=True, tm=min(512, l))
    _, cnk, cnv, _, cgk, cgv, _, clf, clb = proj(ctx, ctx_mods, rotate=False, tm=min(512, n_ctx))

    zero_state = jnp.zeros((b, GLA_HEADS, GLA_VAL_DIM, LANES), F32)
    tb_ctx, tb = min(512, n_ctx), min(1024, l)
    _, s_fwd = _gla_scan(cgk, cgk, clf, cgv, zero_state, reverse=False, tb=tb_ctx)
    _, s_bwd = _gla_scan(cgk, cgk, clb, cgv, zero_state, reverse=True, tb=tb_ctx)
    o_fwd, _ = _gla_scan(gq, gk, lf, gv, s_fwd, reverse=False, tb=tb)
    gla_out, _ = _gla_scan(gq, gk, lb, gv, s_bwd, reverse=True, tb=tb, prev=o_fwd, gate=gr, gain=gla_norm)

    na_out = _na_attention(naq, nak, nav, cnk, cnv, _na_bias_table(na_rpb[0]), rows_per_step=min(16, l // GRID_W))

    gains = jnp.concatenate([norm_post_mix, norm_pre_ffn, norm_post_ffn], axis=0)
    return _out_ffn(x, na_out, gla_out, lat_mods, gains, w_out[0].astype(BF16), w_gate_up[0].astype(BF16),
                    w_down[0].astype(BF16), tm=min(512, l))
```

```python
import functools

import numpy as np
import jax
import jax.numpy as jnp
from jax import lax
from jax.experimental import pallas as pl
from jax.experimental.pallas import tpu as pltpu

F32 = jnp.float32
BF16 = jnp.bfloat16

GRID_W = 64
NA_HEADS = 8
NA_HEAD_DIM = 64
NA_WIN_ROWS = 8
NA_WIN_COLS = 16
GLA_HEADS = 4
GLA_KEY_DIM = 64
GLA_VAL_DIM = 128
GLA_GATE_RANK = 16
GLA_GATE_TAU = 16.0
GLA_CHUNK = 64
ROPE_BASE = 10000.0
NORM_EPS = 1e-6

NA_WIDTH = NA_HEADS * NA_HEAD_DIM
GLA_QK_WIDTH = GLA_HEADS * GLA_KEY_DIM
GLA_V_WIDTH = GLA_HEADS * GLA_VAL_DIM

LANES = 128
MASK_VALUE = -1e30
LOG2_E = 1.4426950408889634
NA_Q_SCALE = NA_HEAD_DIM ** -0.5 * LOG2_E
VMEM_LIMIT_BYTES = 56 * 1024 * 1024

NT_DIMS = (((1,), (1,)), ((), ()))
TN_DIMS = (((0,), (0,)), ((), ()))


def _dot(a, b):
    return jnp.dot(a, b, preferred_element_type=F32)


def _dot_nt(a, b):
    return lax.dot_general(a, b, NT_DIMS, preferred_element_type=F32)


def _dot_tn(a, b):
    return lax.dot_general(a, b, TN_DIMS, preferred_element_type=F32)


def _split_bf16(a):
    hi = a.astype(BF16)
    lo = (a - hi.astype(F32)).astype(BF16)
    return hi, lo


def _sigmoid(x):
    return 1.0 / (1.0 + jnp.exp(-x))


def _rms(x):
    return x * lax.rsqrt(jnp.mean(x * x, axis=-1, keepdims=True) + NORM_EPS)


def _params(*semantics):
    return pltpu.CompilerParams(dimension_semantics=semantics, vmem_limit_bytes=VMEM_LIMIT_BYTES)


def _resident(shape):
    zeros = (0,) * len(shape)
    return pl.BlockSpec(shape, lambda *_: zeros)


def _mod_kernel(c_ref, w_ref, b_ref, o_ref):
    c = c_ref[...]
    s_hi, s_lo = _split_bf16(c * _sigmoid(c))
    w_hi, w_lo = _split_bf16(w_ref[...])
    o_ref[...] = _dot(s_hi, w_hi) + (_dot(s_hi, w_lo) + _dot(s_lo, w_hi)) + b_ref[...]


def _adaln_mod(cond, w_mod, b_mod):
    r, d = cond.shape
    n = w_mod.shape[1]
    tn = 1024
    return pl.pallas_call(
        _mod_kernel,
        grid=(n // tn,),
        in_specs=[pl.BlockSpec((r, d), lambda j: (0, 0)),
                  pl.BlockSpec((d, tn), lambda j: (0, j)),
                  pl.BlockSpec((1, tn), lambda j: (0, j))],
        out_specs=pl.BlockSpec((r, tn), lambda j: (0, j)),
        out_shape=jax.ShapeDtypeStruct((r, n), F32),
        compiler_params=_params("arbitrary"),
        name="adaln_mod",
    )(cond, w_mod, b_mod)


def _rope_tables(seq_len, rotate):
    if not rotate:
        return jnp.ones((seq_len, LANES), F32), jnp.zeros((seq_len, LANES), F32)
    quarter = GLA_KEY_DIM // 4
    t = jnp.arange(seq_len)
    inv = ROPE_BASE ** (-jnp.arange(quarter, dtype=F32) / quarter)
    ang_row = (t // GRID_W).astype(F32)[:, None] * inv[None, :]
    ang_col = (t % GRID_W).astype(F32)[:, None] * inv[None, :]
    cos = jnp.concatenate([jnp.cos(ang_row)] * 2 + [jnp.cos(ang_col)] * 2, axis=-1)
    sin = jnp.concatenate([-jnp.sin(ang_row), jnp.sin(ang_row), -jnp.sin(ang_col), jnp.sin(ang_col)], axis=-1)
    reps = LANES // GLA_KEY_DIM
    return jnp.tile(cos, (1, reps)), jnp.tile(sin, (1, reps))


def _in_proj_kernel(x_ref, mod_ref, g_ref, w_ref, wa_ref, ba_ref, cos_ref, sin_ref,
                    naq_ref, nak_ref, nav_ref, gq_ref, gk_ref, gv_ref, gr_ref, lf_ref, lb_ref, *, sub_tiles):
    shift, scale = mod_ref[0:1, :], mod_ref[1:2, :]
    ts = x_ref.shape[0] // sub_tiles
    quarter = GLA_KEY_DIM // 4
    lane = lax.broadcasted_iota(jnp.int32, (ts, LANES), 1)
    first_half = (lane & quarter) == 0
    hb = [None] * sub_tiles

    def normalize(s):
        x = x_ref[s * ts:(s + 1) * ts, :]
        hb[s] = ((_rms(x) * g_ref[...]) * (1.0 + scale) + shift).astype(BF16)

    def project(s, after_first_dot):
        rows = slice(s * ts, (s + 1) * ts)

        def proj(c0, width):
            return _dot(hb[s], w_ref[:, c0:c0 + width])

        c_gr = 3 * NA_WIDTH + 2 * GLA_QK_WIDTH + GLA_V_WIDTH
        gr_gate = proj(c_gr, GLA_V_WIDTH + LANES)
        after_first_dot()
        gr_ref[rows, :] = gr_gate[:, :GLA_V_WIDTH].astype(BF16)
        low_rank = gr_gate[:, GLA_V_WIDTH:].astype(BF16)
        z = _dot(low_rank, wa_ref[...]) + ba_ref[...]
        log_gate = (jnp.minimum(z, 0.0) - jnp.log(1.0 + jnp.exp(-jnp.abs(z)))) * (1.0 / GLA_GATE_TAU)
        lf_ref[rows, :] = log_gate[:, :GLA_QK_WIDTH]
        lb_ref[rows, :] = log_gate[:, GLA_QK_WIDTH:]

        naq_ref[rows, :] = (proj(0, NA_WIDTH) * NA_Q_SCALE).astype(BF16)
        nak_ref[rows, :] = proj(NA_WIDTH, NA_WIDTH).astype(BF16)
        nav_ref[rows, :] = proj(2 * NA_WIDTH, NA_WIDTH).astype(BF16)

        cos, sin = cos_ref[rows, :], sin_ref[rows, :]

        def rope(t):
            partner = jnp.where(first_half, pltpu.roll(t, LANES - quarter, 1), pltpu.roll(t, quarter, 1))
            return t * cos + partner * sin

        c_gq = 3 * NA_WIDTH
        gqk = proj(c_gq, 2 * GLA_QK_WIDTH)
        for j in range(GLA_QK_WIDTH // LANES):
            lanes = slice(j * LANES, (j + 1) * LANES)
            k_lanes = slice(GLA_QK_WIDTH + j * LANES, GLA_QK_WIDTH + (j + 1) * LANES)
            gq_ref[rows, lanes] = (rope(gqk[:, lanes]) * GLA_KEY_DIM ** -0.5).astype(BF16)
            gk_ref[rows, lanes] = rope(gqk[:, k_lanes]).astype(BF16)

        gv_ref[rows, :] = proj(c_gq + 2 * GLA_QK_WIDTH, GLA_V_WIDTH).astype(BF16)

    normalize(0)
    for s in range(sub_tiles):
        project(s, (lambda nxt=s + 1: normalize(nxt)) if s + 1 < sub_tiles else (lambda: None))


def _in_proj(x, mods, gain, w_all, w_a2, b_a2, rotate, tm, sub_tiles):
    b, l, d = x.shape
    per_batch = mods.shape[0] > 1
    cos, sin = _rope_tables(l, rotate)
    tok = lambda width: pl.BlockSpec((None, tm, width), lambda bi, i: (bi, i, 0))
    out_widths = [NA_WIDTH] * 3 + [GLA_QK_WIDTH] * 2 + [GLA_V_WIDTH] * 2 + [GLA_QK_WIDTH] * 2
    out_dtypes = [BF16] * 7 + [F32] * 2
    return pl.pallas_call(
        functools.partial(_in_proj_kernel, sub_tiles=sub_tiles),
        grid=(b, l // tm),
        in_specs=[tok(d),
                  pl.BlockSpec((None,) + mods.shape[1:], lambda bi, i: (bi if per_batch else 0, 0, 0)),
                  _resident(gain.shape), _resident(w_all.shape),
                  _resident(w_a2.shape), _resident(b_a2.shape),
                  pl.BlockSpec((tm, LANES), lambda bi, i: (i, 0)),
                  pl.BlockSpec((tm, LANES), lambda bi, i: (i, 0))],
        out_specs=[tok(w) for w in out_widths],
        out_shape=[jax.ShapeDtypeStruct((b, l, w), dt) for w, dt in zip(out_widths, out_dtypes)],
        compiler_params=_params("parallel", "parallel"),
        name="in_proj",
    )(x, mods, gain, w_all, w_a2, b_a2, cos, sin)


def _gla_kernel(*refs, reverse, finalize, chunks):
    if finalize:
        q_ref, k_ref, lg_ref, v_ref, s0_ref, prev_ref, gr_ref, gain_ref, out_ref, sfin_ref, st_ref = refs
    else:
        q_ref, k_ref, lg_ref, v_ref, s0_ref, out_ref, sfin_ref, st_ref = refs
    step = pl.program_id(2)

    @pl.when(step == 0)
    def _():
        st_ref[...] = s0_ref[...]

    c = GLA_CHUNK
    row = lax.broadcasted_iota(jnp.int32, (c, c), 0)
    col = lax.broadcasted_iota(jnp.int32, (c, c), 1)
    tri = (row <= col) if reverse else (row >= col)
    tri_b = tri.astype(BF16)
    lane = lax.broadcasted_iota(jnp.int32, (c, LANES), 1)
    head_lanes = [lane < GLA_KEY_DIM, lane >= GLA_KEY_DIM]
    end_row = 0 if reverse else c - 1

    heads = range(2)
    rows = [slice(i * c, (i + 1) * c) for i in range(chunks)]
    vals = [slice(h * GLA_VAL_DIM, (h + 1) * GLA_VAL_DIM) for h in heads]
    order = list(reversed(range(chunks))) if reverse else list(range(chunks))

    lg_hi, lg_lo = _split_bf16(lg_ref[...])
    side_by_side = lambda a: jnp.concatenate([a[r, :] for r in rows], axis=1)
    bcum_all = _dot(tri_b, side_by_side(lg_hi)) + _dot(tri_b, side_by_side(lg_lo))
    bcum = [bcum_all[:, i * LANES:(i + 1) * LANES] for i in range(chunks)]

    q_heads, k_inv, k_end, decay = [], [], [], []
    for i in range(chunks):
        b_end = bcum[i][end_row:end_row + 1, :]
        q_dec = q_ref[rows[i], :].astype(F32) * jnp.exp(bcum[i])
        k = k_ref[rows[i], :].astype(F32)
        q_heads.append([jnp.where(head_lanes[h], q_dec, 0.0).astype(BF16) for h in heads])
        k_inv.append((k * jnp.exp(-bcum[i])).astype(BF16))
        k_end.append((k * jnp.exp(b_end - bcum[i])).astype(BF16))
        decay.append(jnp.exp(b_end))

    v_t = [[v_ref[rows[i], vals[h]].T for h in heads] for i in range(chunks)]
    attn = [[jnp.where(tri, _dot_nt(q_heads[i][h], k_inv[i]), 0.0).astype(BF16) for h in heads] for i in range(chunks)]
    kv = [[_dot(v_t[i][h], k_end[i]) for h in heads] for i in range(chunks)]

    state_before = [[None, None] for _ in range(chunks)]
    for h in heads:
        st = st_ref[h]
        for i in order:
            state_before[i][h] = st.astype(BF16)
            st = st * decay[i] + kv[i][h]
        st_ref[h] = st

    for i in range(chunks):
        for h in heads:
            lhs = jnp.concatenate([q_heads[i][h], attn[i][h]], axis=1)
            rhs = jnp.concatenate([state_before[i][h], v_t[i][h]], axis=1)
            o = _dot_nt(lhs, rhs)
            if finalize:
                o = _rms(o + prev_ref[rows[i], vals[h]])
                gate = gr_ref[rows[i], vals[h]].astype(F32)
                out_ref[rows[i], vals[h]] = (o * gain_ref[:, vals[h]] * (gate * _sigmoid(gate))).astype(out_ref.dtype)
            else:
                out_ref[rows[i], vals[h]] = o

    @pl.when(step == pl.num_programs(2) - 1)
    def _():
        sfin_ref[...] = st_ref[...]


def _gla_scan(q, k, lg, v, s0, *, reverse, tb, prev=None, gate=None, gain=None):
    b, l, _ = q.shape
    finalize = prev is not None
    nb = l // tb
    pairs = GLA_HEADS // 2
    blk = (lambda i: nb - 1 - i) if reverse else (lambda i: i)
    qk_spec = pl.BlockSpec((None, tb, LANES), lambda bi, p, i: (bi, blk(i), p))
    v_spec = pl.BlockSpec((None, tb, 2 * GLA_VAL_DIM), lambda bi, p, i: (bi, blk(i), p))
    st_spec = pl.BlockSpec((None, 2, GLA_VAL_DIM, LANES), lambda bi, p, i: (bi, p, 0, 0))
    in_specs = [qk_spec, qk_spec, qk_spec, v_spec, st_spec]
    args = [q, k, lg, v, s0]
    if finalize:
        in_specs += [v_spec, v_spec, pl.BlockSpec((1, 2 * GLA_VAL_DIM), lambda bi, p, i: (0, p))]
        args += [prev, gate, gain]
    kernel = functools.partial(_gla_kernel, reverse=reverse, finalize=finalize, chunks=tb // GLA_CHUNK)
    return pl.pallas_call(
        kernel,
        grid=(b, pairs, nb),
        in_specs=in_specs,
        out_specs=[v_spec, st_spec],
        out_shape=[jax.ShapeDtypeStruct((b, l, GLA_V_WIDTH), BF16 if finalize else F32),
                   jax.ShapeDtypeStruct(s0.shape, F32)],
        scratch_shapes=[pltpu.VMEM((2, GLA_VAL_DIM, LANES), F32)],
        compiler_params=_params("parallel", "parallel", "arbitrary"),
        name="gla_scan_bwd" if reverse else "gla_scan_fwd",
    )(*args)


def _na_bias_table(rpb):
    kr, kw, w = NA_WIN_ROWS, NA_WIN_COLS, GRID_W
    n_rows = 2 * kr - 1
    cols = np.arange(w)
    col_start = np.clip(cols - kw // 2, 0, w - kw)
    in_window = (cols[None, :] >= col_start[:, None]) & (cols[None, :] < col_start[:, None] + kw)
    lead = w - kw
    padded = jnp.pad(rpb, ((0, 0), (0, 0), (lead, 2 * w - lead - (2 * kw - 1))))
    flat = jnp.broadcast_to(padded[:, :, None, :], (NA_HEADS, n_rows, w, 2 * w)).reshape(NA_HEADS, n_rows, 2 * w * w)
    toe = flat[:, :, w - 1:w - 1 + w * (2 * w - 1)].reshape(NA_HEADS, n_rows, w, 2 * w - 1)[..., :w]
    toe = jnp.where(in_window[None, None], toe * LOG2_E, MASK_VALUE).transpose(0, 2, 1, 3)
    return jnp.stack([toe[:, :, kr - 1 - d:2 * kr - 1 - d, :].reshape(NA_HEADS, w, kr * w) for d in range(kr)], axis=1)


def _na_kernel(q_ref, k_ref, v_ref, kc_ref, vc_ref, bias_ref, o_ref, *, rows_per_step, grid_rows):
    kr, w = NA_WIN_ROWS, GRID_W
    band = kr * w
    lane = lax.broadcasted_iota(jnp.int32, (w, LANES), 1)
    first_head = lane < NA_HEAD_DIM
    base_row = pl.program_id(2) * rows_per_step
    steps = range(rows_per_step)

    stacked = []
    for i in steps:
        q = q_ref[i * w:(i + 1) * w, :]
        zero = jnp.zeros_like(q)
        stacked += [jnp.where(first_head, q, zero), jnp.where(first_head, zero, q)]
    q2 = jnp.concatenate(stacked, axis=0)
    pair = [slice(i * 2 * w, (i + 1) * 2 * w) for i in steps]

    s_ctx = _dot_nt(q2, kc_ref[...])
    band_rows, s_loc = [], []
    for i in steps:
        r = base_row + i
        start = jnp.clip(r - kr // 2, 0, grid_rows - kr)
        band_rows.append(pl.ds(pl.multiple_of(start * w, w), band))
        bias = jnp.concatenate([bias_ref[0, r - start], bias_ref[1, r - start]], axis=0)
        s_loc.append(_dot_nt(q2[pair[i], :], k_ref[band_rows[i], :]) + bias)

    m = jnp.max(s_ctx, axis=-1, keepdims=True)
    p_loc, p_ctx, denom = [], [], []
    for i in steps:
        m_i = jnp.maximum(jnp.max(s_loc[i], axis=-1, keepdims=True), m[pair[i], :])
        p = jnp.exp2(s_loc[i] - m_i)
        p_c = jnp.exp2(s_ctx[pair[i], :] - m_i)
        denom.append(jnp.sum(p, axis=-1, keepdims=True) + jnp.sum(p_c, axis=-1, keepdims=True))
        p_loc.append(p.astype(BF16))
        p_ctx.append(p_c.astype(BF16))
    pv_ctx = _dot(jnp.concatenate(p_ctx, axis=0), vc_ref[...])

    for i in steps:
        pv = (_dot(p_loc[i], v_ref[band_rows[i], :]) + pv_ctx[pair[i], :]) * (1.0 / denom[i])
        o_ref[i * w:(i + 1) * w, :] = jnp.where(first_head, pv[:w], pv[w:]).astype(o_ref.dtype)


def _na_attention(q, k, v, k_ctx, v_ctx, bias, rows_per_step):
    b, l, _ = q.shape
    n_ctx = k_ctx.shape[1]
    grid_rows = l // GRID_W
    tq = rows_per_step * GRID_W
    pairs = NA_HEADS // 2
    q_spec = pl.BlockSpec((None, tq, LANES), lambda bi, p, i: (bi, i, p))
    seq_spec = pl.BlockSpec((None, l, LANES), lambda bi, p, i: (bi, 0, p))
    ctx_spec = pl.BlockSpec((None, n_ctx, LANES), lambda bi, p, i: (bi, 0, p))
    bias_spec = pl.BlockSpec((2,) + bias.shape[1:], lambda bi, p, i: (p, 0, 0, 0))
    kernel = functools.partial(_na_kernel, rows_per_step=rows_per_step, grid_rows=grid_rows)
    return pl.pallas_call(
        kernel,
        grid=(b, pairs, grid_rows // rows_per_step),
        in_specs=[q_spec, seq_spec, seq_spec, ctx_spec, ctx_spec, bias_spec],
        out_specs=q_spec,
        out_shape=jax.ShapeDtypeStruct(q.shape, BF16),
        compiler_params=_params("parallel", "parallel", "arbitrary"),
        name="na_attention",
    )(q, k, v, k_ctx, v_ctx, bias)


def _out_ffn_kernel(x_ref, na_ref, gla_ref, mod_ref, gains_ref, wo_ref, wgu_ref, wd_ref, o_ref, *,
                    hidden_chunks, sub_tiles):
    gate_mix, shift_ffn, scale_ffn, gate_ffn = (mod_ref[i:i + 1, :] for i in (2, 3, 4, 5))
    g_post_mix, g_pre_ffn, g_post_ffn = (gains_ref[i:i + 1, :] for i in (0, 1, 2))
    half = na_ref.shape[-1]
    hidden = wd_ref.shape[0]
    ts = x_ref.shape[0] // sub_tiles
    rows = [slice(s * ts, (s + 1) * ts) for s in range(sub_tiles)]
    x1, h, acc = [None] * sub_tiles, [None] * sub_tiles, [None] * sub_tiles

    y = [_dot(na_ref[r, :], wo_ref[:half, :]) + _dot(gla_ref[r, :], wo_ref[half:, :]) for r in rows]

    def mix(s):
        x1[s] = x_ref[rows[s], :] + gate_mix * (_rms(y[s]) * g_post_mix)
        h[s] = ((_rms(x1[s]) * g_pre_ffn) * (1.0 + scale_ffn) + shift_ffn).astype(BF16)

    def ffn_chunk(s, c0, width):
        g = _dot(h[s], wgu_ref[:, c0:c0 + width])
        u = _dot(h[s], wgu_ref[:, hidden + c0:hidden + c0 + width])
        a = ((g * _sigmoid(g)) * u).astype(BF16)
        part = _dot(a, wd_ref[c0:c0 + width, :])
        acc[s] = part if acc[s] is None else acc[s] + part

    def finish(s):
        o_ref[rows[s], :] = x1[s] + gate_ffn * (_rms(acc[s]) * g_post_ffn)

    mix(0)
    for s in range(sub_tiles):
        for ci, (c0, width) in enumerate(hidden_chunks):
            ffn_chunk(s, c0, width)
            if ci == 0 and s + 1 < sub_tiles:
                mix(s + 1)
            if ci == 0 and s > 0:
                finish(s - 1)
    finish(sub_tiles - 1)


def _out_ffn(x, na_out, gla_out, mods, gains, w_out, w_gate_up, w_down, tm, sub_tiles):
    b, l, d = x.shape
    hidden = w_down.shape[0]
    chunk = 1024
    hidden_chunks = tuple((c0, min(chunk, hidden - c0)) for c0 in range(0, hidden, chunk))
    tok = lambda width: pl.BlockSpec((None, tm, width), lambda bi, i: (bi, i, 0))
    kernel = functools.partial(_out_ffn_kernel, hidden_chunks=hidden_chunks, sub_tiles=sub_tiles)
    return pl.pallas_call(
        kernel,
        grid=(b, l // tm),
        in_specs=[tok(d), tok(na_out.shape[-1]), tok(gla_out.shape[-1]),
                  pl.BlockSpec((None,) + mods.shape[1:], lambda bi, i: (bi, 0, 0)),
                  _resident(gains.shape), _resident(w_out.shape), _resident(w_gate_up.shape),
                  _resident(w_down.shape)],
        out_specs=tok(d),
        out_shape=jax.ShapeDtypeStruct(x.shape, x.dtype),
        compiler_params=_params("parallel", "parallel"),
        name="out_ffn",
    )(x, na_out, gla_out, mods, gains, w_out, w_gate_up, w_down)


def kernel(x, c, ctx, c_ctx, w_mod, b_mod, norm_pre_mix, norm_post_mix, norm_pre_ffn, norm_post_ffn, w_in, na_rpb, gla_wa2_f, gla_ba_f, gla_wa2_b, gla_ba_b, gla_norm, w_out, w_gate_up, w_down):
    assert w_mod.shape[0] == 1, "single-layer stack: the context stream is only read, never updated"
    b, l, d = x.shape
    n_ctx = ctx.shape[1]
    rank = GLA_GATE_RANK

    cond = jnp.concatenate([c, c_ctx[None, :]], axis=0)
    cond = jnp.pad(cond, ((0, -cond.shape[0] % 8), (0, 0)))
    mods = _adaln_mod(cond, w_mod[0], b_mod[0][None, :])
    lat_mods = mods[:b].reshape(b, 6, d)
    ctx_mods = mods[b:b + 1].reshape(1, 6, d)

    w_all = jnp.pad(w_in[0], ((0, 0), (0, LANES - 2 * rank))).astype(BF16)
    w_a2 = jnp.zeros((LANES, 2 * GLA_QK_WIDTH), F32)
    w_a2 = w_a2.at[:rank, :GLA_QK_WIDTH].set(gla_wa2_f[0]).at[rank:2 * rank, GLA_QK_WIDTH:].set(gla_wa2_b[0])
    w_a2 = w_a2.astype(BF16)
    b_a2 = jnp.concatenate([gla_ba_f[0], gla_ba_b[0]])[None, :]

    proj = functools.partial(_in_proj, gain=norm_pre_mix, w_all=w_all, w_a2=w_a2, b_a2=b_a2)
    naq, nak, nav, gq, gk, gv, gr, lf, lb = proj(x, lat_mods, rotate=True, tm=min(1024, l), sub_tiles=4)
    _, cnk, cnv, _, cgk, cgv, _, clf, clb = proj(ctx, ctx_mods, rotate=False, tm=min(512, n_ctx), sub_tiles=1)

    zero_state = jnp.zeros((b, GLA_HEADS, GLA_VAL_DIM, LANES), F32)
    tb_ctx, tb = min(512, n_ctx), min(1024, l)
    _, s_fwd = _gla_scan(cgk, cgk, clf, cgv, zero_state, reverse=False, tb=tb_ctx)
    _, s_bwd = _gla_scan(cgk, cgk, clb, cgv, zero_state, reverse=True, tb=tb_ctx)
    o_fwd, _ = _gla_scan(gq, gk, lf, gv, s_fwd, reverse=False, tb=tb)
    gla_out, _ = _gla_scan(gq, gk, lb, gv, s_bwd, reverse=True, tb=tb, prev=o_fwd, gate=gr, gain=gla_norm)

    na_out = _na_attention(naq, nak, nav, cnk, cnv, _na_bias_table(na_rpb[0]), rows_per_step=min(16, l // GRID_W))

    gains = jnp.concatenate([norm_post_mix, norm_pre_ffn, norm_post_ffn], axis=0)
    return _out_ffn(x, na_out, gla_out, lat_mods, gains, w_out[0].astype(BF16), w_gate_up[0].astype(BF16),
                    w_down[0].astype(BF16), tm=min(1024, l), sub_tiles=4)
```

```python
import functools

import numpy as np
import jax
import jax.numpy as jnp
from jax import lax
from jax.experimental import pallas as pl
from jax.experimental.pallas import tpu as pltpu

F32 = jnp.float32
BF16 = jnp.bfloat16

GRID_W = 64
NA_HEADS = 8
NA_HEAD_DIM = 64
NA_WIN_ROWS = 8
NA_WIN_COLS = 16
GLA_HEADS = 4
GLA_KEY_DIM = 64
GLA_VAL_DIM = 128
GLA_GATE_RANK = 16
GLA_GATE_TAU = 16.0
GLA_CHUNK = 64
ROPE_BASE = 10000.0
NORM_EPS = 1e-6

NA_WIDTH = NA_HEADS * NA_HEAD_DIM
GLA_QK_WIDTH = GLA_HEADS * GLA_KEY_DIM
GLA_V_WIDTH = GLA_HEADS * GLA_VAL_DIM

LANES = 128
MASK_VALUE = -1e30
LOG2_E = 1.4426950408889634
NA_Q_SCALE = NA_HEAD_DIM ** -0.5 * LOG2_E
VMEM_LIMIT_BYTES = 56 * 1024 * 1024

PROJ_TILE, PROJ_SUB_TILES = 1024, 4
CTX_PROJ_TILE = 512
FFN_TILE, FFN_SUB_TILES = 1024, 4
GLA_BLOCK, GLA_GROUP = 4096, 2
NA_ROWS_PER_STEP, NA_GROUP = 32, 1

NT_DIMS = (((1,), (1,)), ((), ()))
TN_DIMS = (((0,), (0,)), ((), ()))


def _dot(a, b):
    return jnp.dot(a, b, preferred_element_type=F32)


def _dot_nt(a, b):
    return lax.dot_general(a, b, NT_DIMS, preferred_element_type=F32)


def _dot_tn(a, b):
    return lax.dot_general(a, b, TN_DIMS, preferred_element_type=F32)


def _split_bf16(a):
    hi = a.astype(BF16)
    lo = (a - hi.astype(F32)).astype(BF16)
    return hi, lo


def _sigmoid(x):
    return 1.0 / (1.0 + jnp.exp(-x))


def _rms(x):
    return x * lax.rsqrt(jnp.mean(x * x, axis=-1, keepdims=True) + NORM_EPS)


def _params(*semantics):
    return pltpu.CompilerParams(dimension_semantics=semantics, vmem_limit_bytes=VMEM_LIMIT_BYTES)


def _resident(shape):
    zeros = (0,) * len(shape)
    return pl.BlockSpec(shape, lambda *_: zeros)


def _mod_kernel(c_ref, w_ref, b_ref, o_ref):
    c = c_ref[...]
    s_hi, s_lo = _split_bf16(c * _sigmoid(c))
    w_hi, w_lo = _split_bf16(w_ref[...])
    o_ref[...] = _dot(s_hi, w_hi) + (_dot(s_hi, w_lo) + _dot(s_lo, w_hi)) + b_ref[...]


def _adaln_mod(cond, w_mod, b_mod):
    r, d = cond.shape
    n = w_mod.shape[1]
    tn = 1024
    return pl.pallas_call(
        _mod_kernel,
        grid=(n // tn,),
        in_specs=[pl.BlockSpec((r, d), lambda j: (0, 0)),
                  pl.BlockSpec((d, tn), lambda j: (0, j)),
                  pl.BlockSpec((1, tn), lambda j: (0, j))],
        out_specs=pl.BlockSpec((r, tn), lambda j: (0, j)),
        out_shape=jax.ShapeDtypeStruct((r, n), F32),
        compiler_params=_params("arbitrary"),
        name="adaln_mod",
    )(cond, w_mod, b_mod)


def _rope_tables(seq_len, rotate):
    if not rotate:
        return jnp.ones((seq_len, LANES), F32), jnp.zeros((seq_len, LANES), F32)
    quarter = GLA_KEY_DIM // 4
    t = jnp.arange(seq_len)
    inv = ROPE_BASE ** (-jnp.arange(quarter, dtype=F32) / quarter)
    ang_row = (t // GRID_W).astype(F32)[:, None] * inv[None, :]
    ang_col = (t % GRID_W).astype(F32)[:, None] * inv[None, :]
    cos = jnp.concatenate([jnp.cos(ang_row)] * 2 + [jnp.cos(ang_col)] * 2, axis=-1)
    sin = jnp.concatenate([-jnp.sin(ang_row), jnp.sin(ang_row), -jnp.sin(ang_col), jnp.sin(ang_col)], axis=-1)
    reps = LANES // GLA_KEY_DIM
    return jnp.tile(cos, (1, reps)), jnp.tile(sin, (1, reps))


def _in_proj_kernel(x_ref, mod_ref, g_ref, w_ref, wa_ref, ba_ref, cos_ref, sin_ref,
                    naq_ref, nak_ref, nav_ref, gq_ref, gk_ref, gv_ref, gr_ref, lf_ref, lb_ref, *, sub_tiles):
    shift, scale = mod_ref[0:1, :], mod_ref[1:2, :]
    ts = x_ref.shape[0] // sub_tiles
    quarter = GLA_KEY_DIM // 4
    lane = lax.broadcasted_iota(jnp.int32, (ts, LANES), 1)
    first_half = (lane & quarter) == 0
    hb = [None] * sub_tiles

    def normalize(s):
        x = x_ref[s * ts:(s + 1) * ts, :]
        hb[s] = ((_rms(x) * g_ref[...]) * (1.0 + scale) + shift).astype(BF16)

    def project(s, after_first_dot):
        rows = slice(s * ts, (s + 1) * ts)

        def proj(c0, width):
            return _dot(hb[s], w_ref[:, c0:c0 + width])

        c_gr = 3 * NA_WIDTH + 2 * GLA_QK_WIDTH + GLA_V_WIDTH
        gr_gate = proj(c_gr, GLA_V_WIDTH + LANES)
        after_first_dot()
        gr_ref[rows, :] = gr_gate[:, :GLA_V_WIDTH].astype(BF16)
        low_rank = gr_gate[:, GLA_V_WIDTH:].astype(BF16)
        z = _dot(low_rank, wa_ref[...]) + ba_ref[...]
        log_gate = (jnp.minimum(z, 0.0) - jnp.log(1.0 + jnp.exp(-jnp.abs(z)))) * (1.0 / GLA_GATE_TAU)
        lf_ref[rows, :] = log_gate[:, :GLA_QK_WIDTH]
        lb_ref[rows, :] = log_gate[:, GLA_QK_WIDTH:]

        naq_ref[rows, :] = (proj(0, NA_WIDTH) * NA_Q_SCALE).astype(BF16)
        nak_ref[rows, :] = proj(NA_WIDTH, NA_WIDTH).astype(BF16)
        nav_ref[rows, :] = proj(2 * NA_WIDTH, NA_WIDTH).astype(BF16)

        cos, sin = cos_ref[rows, :], sin_ref[rows, :]

        def rope(t):
            partner = jnp.where(first_half, pltpu.roll(t, LANES - quarter, 1), pltpu.roll(t, quarter, 1))
            return t * cos + partner * sin

        c_gq = 3 * NA_WIDTH
        gqk = proj(c_gq, 2 * GLA_QK_WIDTH)
        for j in range(GLA_QK_WIDTH // LANES):
            lanes = slice(j * LANES, (j + 1) * LANES)
            k_lanes = slice(GLA_QK_WIDTH + j * LANES, GLA_QK_WIDTH + (j + 1) * LANES)
            gq_ref[rows, lanes] = (rope(gqk[:, lanes]) * GLA_KEY_DIM ** -0.5).astype(BF16)
            gk_ref[rows, lanes] = rope(gqk[:, k_lanes]).astype(BF16)

        gv_ref[rows, :] = proj(c_gq + 2 * GLA_QK_WIDTH, GLA_V_WIDTH).astype(BF16)

    normalize(0)
    for s in range(sub_tiles):
        project(s, (lambda nxt=s + 1: normalize(nxt)) if s + 1 < sub_tiles else (lambda: None))


def _in_proj(x, mods, gain, w_all, w_a2, b_a2, rotate, tm, sub_tiles):
    b, l, d = x.shape
    per_batch = mods.shape[0] > 1
    cos, sin = _rope_tables(l, rotate)
    tok = lambda width: pl.BlockSpec((None, tm, width), lambda bi, i: (bi, i, 0))
    out_widths = [NA_WIDTH] * 3 + [GLA_QK_WIDTH] * 2 + [GLA_V_WIDTH] * 2 + [GLA_QK_WIDTH] * 2
    out_dtypes = [BF16] * 7 + [F32] * 2
    return pl.pallas_call(
        functools.partial(_in_proj_kernel, sub_tiles=sub_tiles),
        grid=(b, l // tm),
        in_specs=[tok(d),
                  pl.BlockSpec((None,) + mods.shape[1:], lambda bi, i: (bi if per_batch else 0, 0, 0)),
                  _resident(gain.shape), _resident(w_all.shape),
                  _resident(w_a2.shape), _resident(b_a2.shape),
                  pl.BlockSpec((tm, LANES), lambda bi, i: (i, 0)),
                  pl.BlockSpec((tm, LANES), lambda bi, i: (i, 0))],
        out_specs=[tok(w) for w in out_widths],
        out_shape=[jax.ShapeDtypeStruct((b, l, w), dt) for w, dt in zip(out_widths, out_dtypes)],
        compiler_params=_params("parallel", "parallel"),
        name="in_proj",
    )(x, mods, gain, w_all, w_a2, b_a2, cos, sin)


def _gla_kernel(*refs, reverse, finalize, chunks, group_size):
    if finalize:
        q_ref, k_ref, lg_ref, v_ref, s0_ref, prev_ref, gr_ref, gain_ref, out_ref, sfin_ref, st_ref = refs
    else:
        q_ref, k_ref, lg_ref, v_ref, s0_ref, out_ref, sfin_ref, st_ref = refs
    step = pl.program_id(2)

    @pl.when(step == 0)
    def _():
        st_ref[...] = s0_ref[...]

    c = GLA_CHUNK
    row = lax.broadcasted_iota(jnp.int32, (c, c), 0)
    col = lax.broadcasted_iota(jnp.int32, (c, c), 1)
    tri = (row <= col) if reverse else (row >= col)
    tri_b = tri.astype(BF16)
    lane = lax.broadcasted_iota(jnp.int32, (c, LANES), 1)
    head_lanes = [lane < GLA_KEY_DIM, lane >= GLA_KEY_DIM]
    end_row = 0 if reverse else c - 1

    heads = range(2)
    rows = [slice(i * c, (i + 1) * c) for i in range(chunks)]
    vals = [slice(h * GLA_VAL_DIM, (h + 1) * GLA_VAL_DIM) for h in heads]
    per_chunk = lambda: [None] * chunks
    bcum, q_heads, k_inv, k_end, decay = per_chunk(), per_chunk(), per_chunk(), per_chunk(), per_chunk()
    v_t, scores, kv, attn, state_before = per_chunk(), per_chunk(), per_chunk(), per_chunk(), per_chunk()
    state = [st_ref[h] for h in heads]

    def log_decay(group):
        halves = [_split_bf16(lg_ref[rows[i], :]) for i in group]
        hi = jnp.concatenate([h_[0] for h_ in halves], axis=1)
        lo = jnp.concatenate([h_[1] for h_ in halves], axis=1)
        both = _dot(tri_b, hi) + _dot(tri_b, lo)
        for n, i in enumerate(group):
            bcum[i] = both[:, n * LANES:(n + 1) * LANES]

    def decayed_operands(group):
        for i in group:
            b_end = bcum[i][end_row:end_row + 1, :]
            q_dec = q_ref[rows[i], :].astype(F32) * jnp.exp(bcum[i])
            k = k_ref[rows[i], :].astype(F32)
            q_heads[i] = [jnp.where(head_lanes[h], q_dec, 0.0).astype(BF16) for h in heads]
            k_inv[i] = (k * jnp.exp(-bcum[i])).astype(BF16)
            k_end[i] = (k * jnp.exp(b_end - bcum[i])).astype(BF16)
            decay[i] = jnp.exp(b_end)

    def chunk_products(group):
        for i in group:
            v_t[i] = [v_ref[rows[i], vals[h]].T for h in heads]
            scores[i] = [_dot_nt(q_heads[i][h], k_inv[i]) for h in heads]
            kv[i] = [_dot(v_t[i][h], k_end[i]) for h in heads]

    def mask_and_carry(group):
        for i in group:
            attn[i] = [jnp.where(tri, scores[i][h], 0.0).astype(BF16) for h in heads]
            state_before[i] = [state[h].astype(BF16) for h in heads]
            for h in heads:
                state[h] = state[h] * decay[i] + kv[i][h]

    def outputs(group):
        for i in group:
            for h in heads:
                lhs = jnp.concatenate([q_heads[i][h], attn[i][h]], axis=1)
                rhs = jnp.concatenate([state_before[i][h], v_t[i][h]], axis=1)
                o = _dot_nt(lhs, rhs)
                if finalize:
                    o = _rms(o + prev_ref[rows[i], vals[h]])
                    gate = gr_ref[rows[i], vals[h]].astype(F32)
                    out_ref[rows[i], vals[h]] = (o * gain_ref[:, vals[h]] * (gate * _sigmoid(gate))).astype(out_ref.dtype)
                else:
                    out_ref[rows[i], vals[h]] = o

    order = list(reversed(range(chunks))) if reverse else list(range(chunks))
    groups = [order[g:g + group_size] for g in range(0, chunks, group_size)]
    stages = [log_decay, decayed_operands, chunk_products, mask_and_carry, outputs]
    for tick in range(len(groups) + len(stages) - 1):
        for k, stage in enumerate(stages):
            if 0 <= tick - k < len(groups):
                stage(groups[tick - k])
    for h in heads:
        st_ref[h] = state[h]

    @pl.when(step == pl.num_programs(2) - 1)
    def _():
        sfin_ref[...] = st_ref[...]


def _gla_scan(q, k, lg, v, s0, *, reverse, tb, group_size, prev=None, gate=None, gain=None):
    b, l, _ = q.shape
    finalize = prev is not None
    nb = l // tb
    pairs = GLA_HEADS // 2
    blk = (lambda i: nb - 1 - i) if reverse else (lambda i: i)
    qk_spec = pl.BlockSpec((None, tb, LANES), lambda bi, p, i: (bi, blk(i), p))
    v_spec = pl.BlockSpec((None, tb, 2 * GLA_VAL_DIM), lambda bi, p, i: (bi, blk(i), p))
    st_spec = pl.BlockSpec((None, 2, GLA_VAL_DIM, LANES), lambda bi, p, i: (bi, p, 0, 0))
    in_specs = [qk_spec, qk_spec, qk_spec, v_spec, st_spec]
    args = [q, k, lg, v, s0]
    if finalize:
        in_specs += [v_spec, v_spec, pl.BlockSpec((1, 2 * GLA_VAL_DIM), lambda bi, p, i: (0, p))]
        args += [prev, gate, gain]
    chunks = tb // GLA_CHUNK
    kernel = functools.partial(_gla_kernel, reverse=reverse, finalize=finalize, chunks=chunks,
                               group_size=min(group_size, chunks))
    return pl.pallas_call(
        kernel,
        grid=(b, pairs, nb),
        in_specs=in_specs,
        out_specs=[v_spec, st_spec],
        out_shape=[jax.ShapeDtypeStruct((b, l, GLA_V_WIDTH), BF16 if finalize else F32),
                   jax.ShapeDtypeStruct(s0.shape, F32)],
        scratch_shapes=[pltpu.VMEM((2, GLA_VAL_DIM, LANES), F32)],
        compiler_params=_params("parallel", "parallel", "arbitrary"),
        name="gla_scan_bwd" if reverse else "gla_scan_fwd",
    )(*args)


def _na_bias_table(rpb):
    kr, kw, w = NA_WIN_ROWS, NA_WIN_COLS, GRID_W
    n_rows = 2 * kr - 1
    cols = np.arange(w)
    col_start = np.clip(cols - kw // 2, 0, w - kw)
    in_window = (cols[None, :] >= col_start[:, None]) & (cols[None, :] < col_start[:, None] + kw)
    lead = w - kw
    padded = jnp.pad(rpb, ((0, 0), (0, 0), (lead, 2 * w - lead - (2 * kw - 1))))
    flat = jnp.broadcast_to(padded[:, :, None, :], (NA_HEADS, n_rows, w, 2 * w)).reshape(NA_HEADS, n_rows, 2 * w * w)
    toe = flat[:, :, w - 1:w - 1 + w * (2 * w - 1)].reshape(NA_HEADS, n_rows, w, 2 * w - 1)[..., :w]
    toe = jnp.where(in_window[None, None], toe * LOG2_E, MASK_VALUE).transpose(0, 2, 1, 3)
    return jnp.stack([toe[:, :, kr - 1 - d:2 * kr - 1 - d, :].reshape(NA_HEADS, w, kr * w) for d in range(kr)], axis=1)


def _na_kernel(q_ref, k_ref, v_ref, kc_ref, vc_ref, bias_ref, o_ref, *, rows_per_step, group_size, grid_rows):
    kr, w = NA_WIN_ROWS, GRID_W
    band = kr * w
    lane = lax.broadcasted_iota(jnp.int32, (w, LANES), 1)
    first_head = lane < NA_HEAD_DIM
    base_row = pl.program_id(2) * rows_per_step
    per_row = lambda: [None] * rows_per_step
    band_rows, s_loc, s_ctx, p_loc, p_ctx, denom = per_row(), per_row(), per_row(), per_row(), per_row(), per_row()
    pair = [slice(n * 2 * w, (n + 1) * 2 * w) for n in range(group_size)]

    def logits(group):
        stacked = []
        for i in group:
            q = q_ref[i * w:(i + 1) * w, :]
            zero = jnp.zeros_like(q)
            stacked += [jnp.where(first_head, q, zero), jnp.where(first_head, zero, q)]
        q2 = jnp.concatenate(stacked, axis=0)
        s_ctx_group = _dot_nt(q2, kc_ref[...])
        for n, i in enumerate(group):
            r = base_row + i
            start = jnp.clip(r - kr // 2, 0, grid_rows - kr)
            band_rows[i] = pl.ds(pl.multiple_of(start * w, w), band)
            bias = jnp.concatenate([bias_ref[0, r - start], bias_ref[1, r - start]], axis=0)
            s_loc[i] = _dot_nt(q2[pair[n], :], k_ref[band_rows[i], :]) + bias
            s_ctx[i] = s_ctx_group[pair[n], :]

    def softmax(group):
        for i in group:
            m = jnp.maximum(jnp.max(s_loc[i], axis=-1, keepdims=True), jnp.max(s_ctx[i], axis=-1, keepdims=True))
            p = jnp.exp2(s_loc[i] - m)
            p_c = jnp.exp2(s_ctx[i] - m)
            denom[i] = jnp.sum(p, axis=-1, keepdims=True) + jnp.sum(p_c, axis=-1, keepdims=True)
            p_loc[i] = p.astype(BF16)
            p_ctx[i] = p_c.astype(BF16)

    def weighted_values(group):
        pv_ctx = _dot(jnp.concatenate([p_ctx[i] for i in group], axis=0), vc_ref[...])
        for n, i in enumerate(group):
            pv = (_dot(p_loc[i], v_ref[band_rows[i], :]) + pv_ctx[pair[n], :]) * (1.0 / denom[i])
            o_ref[i * w:(i + 1) * w, :] = jnp.where(first_head, pv[:w], pv[w:]).astype(o_ref.dtype)

    groups = [list(range(g, g + group_size)) for g in range(0, rows_per_step, group_size)]
    stages = [logits, softmax, weighted_values]
    for tick in range(len(groups) + len(stages) - 1):
        for k, stage in enumerate(stages):
            if 0 <= tick - k < len(groups):
                stage(groups[tick - k])


def _na_attention(q, k, v, k_ctx, v_ctx, bias, rows_per_step, group_size):
    b, l, _ = q.shape
    n_ctx = k_ctx.shape[1]
    grid_rows = l // GRID_W
    tq = rows_per_step * GRID_W
    pairs = NA_HEADS // 2
    q_spec = pl.BlockSpec((None, tq, LANES), lambda bi, p, i: (bi, i, p))
    seq_spec = pl.BlockSpec((None, l, LANES), lambda bi, p, i: (bi, 0, p))
    ctx_spec = pl.BlockSpec((None, n_ctx, LANES), lambda bi, p, i: (bi, 0, p))
    bias_spec = pl.BlockSpec((2,) + bias.shape[1:], lambda bi, p, i: (p, 0, 0, 0))
    kernel = functools.partial(_na_kernel, rows_per_step=rows_per_step, group_size=min(group_size, rows_per_step),
                               grid_rows=grid_rows)
    return pl.pallas_call(
        kernel,
        grid=(b, pairs, grid_rows // rows_per_step),
        in_specs=[q_spec, seq_spec, seq_spec, ctx_spec, ctx_spec, bias_spec],
        out_specs=q_spec,
        out_shape=jax.ShapeDtypeStruct(q.shape, BF16),
        compiler_params=_params("parallel", "parallel", "arbitrary"),
        name="na_attention",
    )(q, k, v, k_ctx, v_ctx, bias)


def _out_ffn_kernel(x_ref, na_ref, gla_ref, mod_ref, gains_ref, wo_ref, wgu_ref, wd_ref, o_ref, *,
                    hidden_chunks, sub_tiles):
    gate_mix, shift_ffn, scale_ffn, gate_ffn = (mod_ref[i:i + 1, :] for i in (2, 3, 4, 5))
    g_post_mix, g_pre_ffn, g_post_ffn = (gains_ref[i:i + 1, :] for i in (0, 1, 2))
    half = na_ref.shape[-1]
    hidden = wd_ref.shape[0]
    ts = x_ref.shape[0] // sub_tiles
    rows = [slice(s * ts, (s + 1) * ts) for s in range(sub_tiles)]
    x1, h, acc = [None] * sub_tiles, [None] * sub_tiles, [None] * sub_tiles

    y = [_dot(na_ref[r, :], wo_ref[:half, :]) + _dot(gla_ref[r, :], wo_ref[half:, :]) for r in rows]

    def mix(s):
        x1[s] = x_ref[rows[s], :] + gate_mix * (_rms(y[s]) * g_post_mix)
        h[s] = ((_rms(x1[s]) * g_pre_ffn) * (1.0 + scale_ffn) + shift_ffn).astype(BF16)

    def ffn_chunk(s, c0, width):
        g = _dot(h[s], wgu_ref[:, c0:c0 + width])
        u = _dot(h[s], wgu_ref[:, hidden + c0:hidden + c0 + width])
        a = ((g * _sigmoid(g)) * u).astype(BF16)
        part = _dot(a, wd_ref[c0:c0 + width, :])
        acc[s] = part if acc[s] is None else acc[s] + part

    def finish(s):
        o_ref[rows[s], :] = x1[s] + gate_ffn * (_rms(acc[s]) * g_post_ffn)

    mix(0)
    for s in range(sub_tiles):
        for ci, (c0, width) in enumerate(hidden_chunks):
            ffn_chunk(s, c0, width)
            if ci == 0 and s + 1 < sub_tiles:
                mix(s + 1)
            if ci == 0 and s > 0:
                finish(s - 1)
    finish(sub_tiles - 1)


def _out_ffn(x, na_out, gla_out, mods, gains, w_out, w_gate_up, w_down, tm, sub_tiles):
    b, l, d = x.shape
    hidden = w_down.shape[0]
    chunk = 1024
    hidden_chunks = tuple((c0, min(chunk, hidden - c0)) for c0 in range(0, hidden, chunk))
    tok = lambda width: pl.BlockSpec((None, tm, width), lambda bi, i: (bi, i, 0))
    kernel = functools.partial(_out_ffn_kernel, hidden_chunks=hidden_chunks, sub_tiles=sub_tiles)
    return pl.pallas_call(
        kernel,
        grid=(b, l // tm),
        in_specs=[tok(d), tok(na_out.shape[-1]), tok(gla_out.shape[-1]),
                  pl.BlockSpec((None,) + mods.shape[1:], lambda bi, i: (bi, 0, 0)),
                  _resident(gains.shape), _resident(w_out.shape), _resident(w_gate_up.shape),
                  _resident(w_down.shape)],
        out_specs=tok(d),
        out_shape=jax.ShapeDtypeStruct(x.shape, x.dtype),
        compiler_params=_params("parallel", "parallel"),
        name="out_ffn",
    )(x, na_out, gla_out, mods, gains, w_out, w_gate_up, w_down)


def kernel(x, c, ctx, c_ctx, w_mod, b_mod, norm_pre_mix, norm_post_mix, norm_pre_ffn, norm_post_ffn, w_in, na_rpb, gla_wa2_f, gla_ba_f, gla_wa2_b, gla_ba_b, gla_norm, w_out, w_gate_up, w_down):
    assert w_mod.shape[0] == 1, "single-layer stack: the context stream is only read, never updated"
    b, l, d = x.shape
    n_ctx = ctx.shape[1]
    rank = GLA_GATE_RANK

    cond = jnp.concatenate([c, c_ctx[None, :]], axis=0)
    cond = jnp.pad(cond, ((0, -cond.shape[0] % 8), (0, 0)))
    mods = _adaln_mod(cond, w_mod[0], b_mod[0][None, :])
    lat_mods = mods[:b].reshape(b, 6, d)
    ctx_mods = mods[b:b + 1].reshape(1, 6, d)

    w_all = jnp.pad(w_in[0], ((0, 0), (0, LANES - 2 * rank))).astype(BF16)
    w_a2 = jnp.zeros((LANES, 2 * GLA_QK_WIDTH), F32)
    w_a2 = w_a2.at[:rank, :GLA_QK_WIDTH].set(gla_wa2_f[0]).at[rank:2 * rank, GLA_QK_WIDTH:].set(gla_wa2_b[0])
    w_a2 = w_a2.astype(BF16)
    b_a2 = jnp.concatenate([gla_ba_f[0], gla_ba_b[0]])[None, :]

    proj = functools.partial(_in_proj, gain=norm_pre_mix, w_all=w_all, w_a2=w_a2, b_a2=b_a2)
    naq, nak, nav, gq, gk, gv, gr, lf, lb = proj(x, lat_mods, rotate=True, tm=min(PROJ_TILE, l),
                                                 sub_tiles=PROJ_SUB_TILES)
    ctx_streams = proj(ctx.reshape(1, b * n_ctx, d), ctx_mods, rotate=False, tm=min(CTX_PROJ_TILE, b * n_ctx),
                       sub_tiles=1)
    _, cnk, cnv, _, cgk, cgv, _, clf, clb = [s.reshape(b, n_ctx, s.shape[-1]) for s in ctx_streams]

    zero_state = jnp.zeros((b, GLA_HEADS, GLA_VAL_DIM, LANES), F32)
    scan = functools.partial(_gla_scan, group_size=GLA_GROUP)
    _, s_fwd = scan(cgk, cgk, clf, cgv, zero_state, reverse=False, tb=min(GLA_BLOCK, n_ctx))
    _, s_bwd = scan(cgk, cgk, clb, cgv, zero_state, reverse=True, tb=min(GLA_BLOCK, n_ctx))
    o_fwd, _ = scan(gq, gk, lf, gv, s_fwd, reverse=False, tb=min(GLA_BLOCK, l))
    gla_out, _ = scan(gq, gk, lb, gv, s_bwd, reverse=True, tb=min(GLA_BLOCK, l), prev=o_fwd, gate=gr, gain=gla_norm)

    na_out = _na_attention(naq, nak, nav, cnk, cnv, _na_bias_table(na_rpb[0]),
                           rows_per_step=min(NA_ROWS_PER_STEP, l // GRID_W), group_size=NA_GROUP)

    gains = jnp.concatenate([norm_post_mix, norm_pre_ffn, norm_post_ffn], axis=0)
    return _out_ffn(x, na_out, gla_out, lat_mods, gains, w_out[0].astype(BF16), w_gate_up[0].astype(BF16),
                    w_down[0].astype(BF16), tm=min(FFN_TILE, l), sub_tiles=FFN_SUB_TILES)
```

```python
import functools

import numpy as np
import jax
import jax.numpy as jnp
from jax import lax
from jax.experimental import pallas as pl
from jax.experimental.pallas import tpu as pltpu

F32 = jnp.float32
BF16 = jnp.bfloat16

GRID_W = 64
NA_HEADS = 8
NA_HEAD_DIM = 64
NA_WIN_ROWS = 8
NA_WIN_COLS = 16
GLA_HEADS = 4
GLA_KEY_DIM = 64
GLA_VAL_DIM = 128
GLA_GATE_RANK = 16
GLA_GATE_TAU = 16.0
GLA_CHUNK = 64
ROPE_BASE = 10000.0
NORM_EPS = 1e-6

NA_WIDTH = NA_HEADS * NA_HEAD_DIM
GLA_QK_WIDTH = GLA_HEADS * GLA_KEY_DIM
GLA_V_WIDTH = GLA_HEADS * GLA_VAL_DIM

LANES = 128
MASK_VALUE = -1e30
LOG2_E = 1.4426950408889634
NA_Q_SCALE = NA_HEAD_DIM ** -0.5 * LOG2_E
VMEM_LIMIT_BYTES = 56 * 1024 * 1024

PROJ_TILE, PROJ_SUB_TILES = 1024, 4
CTX_PROJ_TILE = 512
FFN_TILE, FFN_SUB_TILES = 1024, 4
GLA_BLOCK, GLA_GROUP = 4096, 2
NA_ROWS_PER_STEP, NA_GROUP = 32, 1

NT_DIMS = (((1,), (1,)), ((), ()))
TN_DIMS = (((0,), (0,)), ((), ()))


def _dot(a, b):
    return jnp.dot(a, b, preferred_element_type=F32)


def _dot_nt(a, b):
    return lax.dot_general(a, b, NT_DIMS, preferred_element_type=F32)


def _dot_tn(a, b):
    return lax.dot_general(a, b, TN_DIMS, preferred_element_type=F32)


def _split_bf16(a):
    hi = a.astype(BF16)
    lo = (a - hi.astype(F32)).astype(BF16)
    return hi, lo


def _sigmoid(x):
    return 1.0 / (1.0 + jnp.exp(-x))


def _rms(x):
    return x * lax.rsqrt(jnp.mean(x * x, axis=-1, keepdims=True) + NORM_EPS)


def _fold_slabs(op, *arrays):
    slabs = [a[:, j:j + LANES] for a in arrays for j in range(0, a.shape[1], LANES)]
    return functools.reduce(op, slabs)


def _params(*semantics):
    return pltpu.CompilerParams(dimension_semantics=semantics, vmem_limit_bytes=VMEM_LIMIT_BYTES)


def _resident(shape):
    zeros = (0,) * len(shape)
    return pl.BlockSpec(shape, lambda *_: zeros)


def _mod_kernel(c_ref, w_ref, b_ref, o_ref):
    c = c_ref[...]
    s_hi, s_lo = _split_bf16(c * _sigmoid(c))
    w_hi, w_lo = _split_bf16(w_ref[...])
    o_ref[...] = _dot(s_hi, w_hi) + (_dot(s_hi, w_lo) + _dot(s_lo, w_hi)) + b_ref[...]


def _adaln_mod(cond, w_mod, b_mod):
    r, d = cond.shape
    n = w_mod.shape[1]
    tn = 1024
    return pl.pallas_call(
        _mod_kernel,
        grid=(n // tn,),
        in_specs=[pl.BlockSpec((r, d), lambda j: (0, 0)),
                  pl.BlockSpec((d, tn), lambda j: (0, j)),
                  pl.BlockSpec((1, tn), lambda j: (0, j))],
        out_specs=pl.BlockSpec((r, tn), lambda j: (0, j)),
        out_shape=jax.ShapeDtypeStruct((r, n), F32),
        compiler_params=_params("arbitrary"),
        name="adaln_mod",
    )(cond, w_mod, b_mod)


def _rope_tables(seq_len, rotate):
    if not rotate:
        return np.ones((seq_len, LANES), np.float32), np.zeros((seq_len, LANES), np.float32)
    quarter = GLA_KEY_DIM // 4
    t = np.arange(seq_len)
    inv = ROPE_BASE ** (-np.arange(quarter, dtype=np.float64) / quarter)
    ang_row = (t // GRID_W)[:, None] * inv[None, :]
    ang_col = (t % GRID_W)[:, None] * inv[None, :]
    cos = np.concatenate([np.cos(ang_row)] * 2 + [np.cos(ang_col)] * 2, axis=-1)
    sin = np.concatenate([-np.sin(ang_row), np.sin(ang_row), -np.sin(ang_col), np.sin(ang_col)], axis=-1)
    reps = LANES // GLA_KEY_DIM
    return np.tile(cos, (1, reps)).astype(np.float32), np.tile(sin, (1, reps)).astype(np.float32)


def _in_proj_kernel(x_ref, mod_ref, g_ref, w_ref, wa_ref, ba_ref, cos_ref, sin_ref,
                    naq_ref, nak_ref, nav_ref, gq_ref, gk_ref, gv_ref, gr_ref, lf_ref, lb_ref, *, sub_tiles):
    shift, scale = mod_ref[0:1, :], mod_ref[1:2, :]
    ts = x_ref.shape[0] // sub_tiles
    quarter = GLA_KEY_DIM // 4
    lane = lax.broadcasted_iota(jnp.int32, (ts, LANES), 1)
    first_half = (lane & quarter) == 0
    hb = [None] * sub_tiles

    def normalize(s):
        x = x_ref[s * ts:(s + 1) * ts, :]
        hb[s] = ((_rms(x) * g_ref[...]) * (1.0 + scale) + shift).astype(BF16)

    def project(s, after_first_dot):
        rows = slice(s * ts, (s + 1) * ts)

        def proj(c0, width):
            return _dot(hb[s], w_ref[:, c0:c0 + width])

        c_gr = 3 * NA_WIDTH + 2 * GLA_QK_WIDTH + GLA_V_WIDTH
        gr_gate = proj(c_gr, GLA_V_WIDTH + LANES)
        after_first_dot()
        gr_ref[rows, :] = gr_gate[:, :GLA_V_WIDTH].astype(BF16)
        low_rank = gr_gate[:, GLA_V_WIDTH:].astype(BF16)
        z = _dot(low_rank, wa_ref[...]) + ba_ref[...]
        log_gate = (jnp.minimum(z, 0.0) - jnp.log(1.0 + jnp.exp(-jnp.abs(z)))) * (1.0 / GLA_GATE_TAU)
        lf_ref[rows, :] = log_gate[:, :GLA_QK_WIDTH]
        lb_ref[rows, :] = log_gate[:, GLA_QK_WIDTH:]

        naq_ref[rows, :] = (proj(0, NA_WIDTH) * NA_Q_SCALE).astype(BF16)
        nak_ref[rows, :] = proj(NA_WIDTH, NA_WIDTH).astype(BF16)
        nav_ref[rows, :] = proj(2 * NA_WIDTH, NA_WIDTH).astype(BF16)

        cos, sin = cos_ref[rows, :], sin_ref[rows, :]

        def rope(t):
            partner = jnp.where(first_half, pltpu.roll(t, LANES - quarter, 1), pltpu.roll(t, quarter, 1))
            return t * cos + partner * sin

        c_gq = 3 * NA_WIDTH
        gqk = proj(c_gq, 2 * GLA_QK_WIDTH)
        for j in range(GLA_QK_WIDTH // LANES):
            lanes = slice(j * LANES, (j + 1) * LANES)
            k_lanes = slice(GLA_QK_WIDTH + j * LANES, GLA_QK_WIDTH + (j + 1) * LANES)
            gq_ref[rows, lanes] = (rope(gqk[:, lanes]) * GLA_KEY_DIM ** -0.5).astype(BF16)
            gk_ref[rows, lanes] = rope(gqk[:, k_lanes]).astype(BF16)

        gv_ref[rows, :] = proj(c_gq + 2 * GLA_QK_WIDTH, GLA_V_WIDTH).astype(BF16)

    normalize(0)
    for s in range(sub_tiles):
        project(s, (lambda nxt=s + 1: normalize(nxt)) if s + 1 < sub_tiles else (lambda: None))


def _in_proj(x, mods, gain, w_all, w_a2, b_a2, rotate, tm, sub_tiles):
    b, l, d = x.shape
    per_batch = mods.shape[0] > 1
    cos, sin = _rope_tables(l, rotate)
    tok = lambda width: pl.BlockSpec((None, tm, width), lambda bi, i: (bi, i, 0))
    out_widths = [NA_WIDTH] * 3 + [GLA_QK_WIDTH] * 2 + [GLA_V_WIDTH] * 2 + [GLA_QK_WIDTH] * 2
    out_dtypes = [BF16] * 7 + [F32] * 2
    return pl.pallas_call(
        functools.partial(_in_proj_kernel, sub_tiles=sub_tiles),
        grid=(b, l // tm),
        in_specs=[tok(d),
                  pl.BlockSpec((None,) + mods.shape[1:], lambda bi, i: (bi if per_batch else 0, 0, 0)),
                  _resident(gain.shape), _resident(w_all.shape),
                  _resident(w_a2.shape), _resident(b_a2.shape),
                  pl.BlockSpec((tm, LANES), lambda bi, i: (i, 0)),
                  pl.BlockSpec((tm, LANES), lambda bi, i: (i, 0))],
        out_specs=[tok(w) for w in out_widths],
        out_shape=[jax.ShapeDtypeStruct((b, l, w), dt) for w, dt in zip(out_widths, out_dtypes)],
        compiler_params=_params("parallel", "parallel"),
        name="in_proj",
    )(x, mods, gain, w_all, w_a2, b_a2, cos, sin)


def _gla_kernel(*refs, reverse, finalize, chunks, group_size):
    if finalize:
        q_ref, k_ref, lg_ref, v_ref, s0_ref, prev_ref, gr_ref, gain_ref, out_ref, sfin_ref, st_ref = refs
    else:
        q_ref, k_ref, lg_ref, v_ref, s0_ref, out_ref, sfin_ref, st_ref = refs
    step = pl.program_id(2)

    @pl.when(step == 0)
    def _():
        st_ref[...] = s0_ref[...]

    c = GLA_CHUNK
    row = lax.broadcasted_iota(jnp.int32, (c, c), 0)
    col = lax.broadcasted_iota(jnp.int32, (c, c), 1)
    tri = (row <= col) if reverse else (row >= col)
    tri_b = tri.astype(BF16)
    lane = lax.broadcasted_iota(jnp.int32, (c, LANES), 1)
    head_lanes = [lane < GLA_KEY_DIM, lane >= GLA_KEY_DIM]
    end_row = 0 if reverse else c - 1

    heads = range(2)
    rows = [slice(i * c, (i + 1) * c) for i in range(chunks)]
    vals = [slice(h * GLA_VAL_DIM, (h + 1) * GLA_VAL_DIM) for h in heads]
    per_chunk = lambda: [None] * chunks
    bcum, q_heads, k_inv, k_end, decay = per_chunk(), per_chunk(), per_chunk(), per_chunk(), per_chunk()
    v_t, scores, kv, attn, state_before = per_chunk(), per_chunk(), per_chunk(), per_chunk(), per_chunk()
    state = [st_ref[h] for h in heads]

    def log_decay(group):
        halves = [_split_bf16(lg_ref[rows[i], :]) for i in group]
        hi = jnp.concatenate([h_[0] for h_ in halves], axis=1)
        lo = jnp.concatenate([h_[1] for h_ in halves], axis=1)
        both = _dot(tri_b, hi) + _dot(tri_b, lo)
        for n, i in enumerate(group):
            bcum[i] = both[:, n * LANES:(n + 1) * LANES]

    def decayed_operands(group):
        for i in group:
            b_end = bcum[i][end_row:end_row + 1, :]
            q_dec = q_ref[rows[i], :].astype(F32) * jnp.exp(bcum[i])
            k = k_ref[rows[i], :].astype(F32)
            q_heads[i] = [jnp.where(head_lanes[h], q_dec, 0.0).astype(BF16) for h in heads]
            k_inv[i] = (k * jnp.exp(-bcum[i])).astype(BF16)
            k_end[i] = (k * jnp.exp(b_end - bcum[i])).astype(BF16)
            decay[i] = jnp.exp(b_end)

    def chunk_products(group):
        for i in group:
            v_t[i] = [v_ref[rows[i], vals[h]].T for h in heads]
            scores[i] = [_dot_nt(q_heads[i][h], k_inv[i]) for h in heads]
            kv[i] = [_dot(v_t[i][h], k_end[i]) for h in heads]

    def mask_and_carry(group):
        for i in group:
            attn[i] = [jnp.where(tri, scores[i][h], 0.0).astype(BF16) for h in heads]
            state_before[i] = [state[h].astype(BF16) for h in heads]
            for h in heads:
                state[h] = state[h] * decay[i] + kv[i][h]

    def outputs(group):
        for i in group:
            for h in heads:
                lhs = jnp.concatenate([q_heads[i][h], attn[i][h]], axis=1)
                rhs = jnp.concatenate([state_before[i][h], v_t[i][h]], axis=1)
                o = _dot_nt(lhs, rhs)
                if finalize:
                    o = _rms(o + prev_ref[rows[i], vals[h]])
                    gate = gr_ref[rows[i], vals[h]].astype(F32)
                    out_ref[rows[i], vals[h]] = (o * gain_ref[:, vals[h]] * (gate * _sigmoid(gate))).astype(out_ref.dtype)
                else:
                    out_ref[rows[i], vals[h]] = o

    order = list(reversed(range(chunks))) if reverse else list(range(chunks))
    groups = [order[g:g + group_size] for g in range(0, chunks, group_size)]
    stages = [log_decay, decayed_operands, chunk_products, mask_and_carry, outputs]
    for tick in range(len(groups) + len(stages) - 1):
        for k, stage in enumerate(stages):
            if 0 <= tick - k < len(groups):
                stage(groups[tick - k])
    for h in heads:
        st_ref[h] = state[h]

    @pl.when(step == pl.num_programs(2) - 1)
    def _():
        sfin_ref[...] = st_ref[...]


def _gla_scan(q, k, lg, v, s0, *, reverse, tb, group_size, prev=None, gate=None, gain=None):
    b, l, _ = q.shape
    finalize = prev is not None
    nb = l // tb
    pairs = GLA_HEADS // 2
    blk = (lambda i: nb - 1 - i) if reverse else (lambda i: i)
    qk_spec = pl.BlockSpec((None, tb, LANES), lambda bi, p, i: (bi, blk(i), p))
    v_spec = pl.BlockSpec((None, tb, 2 * GLA_VAL_DIM), lambda bi, p, i: (bi, blk(i), p))
    st_spec = pl.BlockSpec((None, 2, GLA_VAL_DIM, LANES), lambda bi, p, i: (bi, p, 0, 0))
    in_specs = [qk_spec, qk_spec, qk_spec, v_spec, st_spec]
    args = [q, k, lg, v, s0]
    if finalize:
        in_specs += [v_spec, v_spec, pl.BlockSpec((1, 2 * GLA_VAL_DIM), lambda bi, p, i: (0, p))]
        args += [prev, gate, gain]
    chunks = tb // GLA_CHUNK
    kernel = functools.partial(_gla_kernel, reverse=reverse, finalize=finalize, chunks=chunks,
                               group_size=min(group_size, chunks))
    return pl.pallas_call(
        kernel,
        grid=(b, pairs, nb),
        in_specs=in_specs,
        out_specs=[v_spec, st_spec],
        out_shape=[jax.ShapeDtypeStruct((b, l, GLA_V_WIDTH), BF16 if finalize else F32),
                   jax.ShapeDtypeStruct(s0.shape, F32)],
        scratch_shapes=[pltpu.VMEM((2, GLA_VAL_DIM, LANES), F32)],
        compiler_params=_params("parallel", "parallel", "arbitrary"),
        name="gla_scan_bwd" if reverse else "gla_scan_fwd",
    )(*args)


def _bias_kernel(rpb_ref, o_ref):
    kw, w = NA_WIN_COLS, GRID_W
    c = lax.broadcasted_iota(jnp.int32, (w, LANES), 0)
    lane = lax.broadcasted_iota(jnp.int32, (w, LANES), 1)
    kc = lane & (w - 1)
    start = jnp.clip(c - kw // 2, 0, w - kw)
    in_window = (kc >= start) & (kc < start + kw)
    second_row = lane >= w
    toeplitz = [pltpu.roll(jnp.broadcast_to(rpb_ref[r:r + 1, :], (w, LANES)), LANES - (w - 1), 1, stride=1, stride_axis=0)
                for r in range(rpb_ref.shape[0])]
    for e in range(o_ref.shape[0]):
        slab = jnp.where(second_row, pltpu.roll(toeplitz[e + 1], w, 1), toeplitz[e])
        o_ref[e] = jnp.where(in_window, slab * LOG2_E, MASK_VALUE)


def _na_bias_table(rpb):
    heads, n_rows, n_cols = rpb.shape
    lead = GRID_W - NA_WIN_COLS
    padded = jnp.pad(rpb, ((0, 0), (0, 0), (lead, LANES - lead - n_cols)))
    return pl.pallas_call(
        _bias_kernel,
        grid=(heads,),
        in_specs=[pl.BlockSpec((None, n_rows, LANES), lambda h: (h, 0, 0))],
        out_specs=pl.BlockSpec((None, n_rows - 1, GRID_W, LANES), lambda h: (h, 0, 0, 0)),
        out_shape=jax.ShapeDtypeStruct((heads, n_rows - 1, GRID_W, LANES), F32),
        compiler_params=_params("parallel"),
        name="na_bias_table",
    )(padded)


def _na_kernel(q_ref, k_ref, v_ref, kc_ref, vc_ref, bias_ref, o_ref, *, rows_per_step, group_size, grid_rows):
    kr, w = NA_WIN_ROWS, GRID_W
    band = kr * w
    lane = lax.broadcasted_iota(jnp.int32, (w, LANES), 1)
    first_head = lane < NA_HEAD_DIM
    base_row = pl.program_id(2) * rows_per_step
    per_row = lambda: [None] * rows_per_step
    band_rows, s_loc, s_ctx, p_loc, p_ctx, denom = per_row(), per_row(), per_row(), per_row(), per_row(), per_row()
    pair = [slice(n * 2 * w, (n + 1) * 2 * w) for n in range(group_size)]

    def logits(group):
        stacked = []
        for i in group:
            q = q_ref[i * w:(i + 1) * w, :]
            zero = jnp.zeros_like(q)
            stacked += [jnp.where(first_head, q, zero), jnp.where(first_head, zero, q)]
        q2 = jnp.concatenate(stacked, axis=0)
        s_ctx_group = _dot_nt(q2, kc_ref[...])
        for n, i in enumerate(group):
            r = base_row + i
            start = jnp.clip(r - kr // 2, 0, grid_rows - kr)
            band_rows[i] = pl.ds(pl.multiple_of(start * w, w), band)
            first_slab = kr - 1 - (r - start)
            bias = jnp.concatenate([jnp.concatenate([bias_ref[h, first_slab + 2 * j] for j in range(kr // 2)], axis=1)
                                    for h in range(2)], axis=0)
            s_loc[i] = _dot_nt(q2[pair[n], :], k_ref[band_rows[i], :]) + bias
            s_ctx[i] = s_ctx_group[pair[n], :]

    def softmax(group):
        for i in group:
            m = jnp.max(_fold_slabs(jnp.maximum, s_loc[i], s_ctx[i]), axis=-1, keepdims=True)
            p = jnp.exp2(s_loc[i] - m)
            p_c = jnp.exp2(s_ctx[i] - m)
            denom[i] = jnp.sum(_fold_slabs(jnp.add, p, p_c), axis=-1, keepdims=True)
            p_loc[i] = p.astype(BF16)
            p_ctx[i] = p_c.astype(BF16)

    def weighted_values(group):
        pv_ctx = _dot(jnp.concatenate([p_ctx[i] for i in group], axis=0), vc_ref[...])
        for n, i in enumerate(group):
            pv = (_dot(p_loc[i], v_ref[band_rows[i], :]) + pv_ctx[pair[n], :]) * (1.0 / denom[i])
            o_ref[i * w:(i + 1) * w, :] = jnp.where(first_head, pv[:w], pv[w:]).astype(o_ref.dtype)

    groups = [list(range(g, g + group_size)) for g in range(0, rows_per_step, group_size)]
    stages = [logits, softmax, weighted_values]
    for tick in range(len(groups) + len(stages) - 1):
        for k, stage in enumerate(stages):
            if 0 <= tick - k < len(groups):
                stage(groups[tick - k])


def _na_attention(q, k, v, k_ctx, v_ctx, bias, rows_per_step, group_size):
    b, l, _ = q.shape
    n_ctx = k_ctx.shape[1]
    grid_rows = l // GRID_W
    tq = rows_per_step * GRID_W
    pairs = NA_HEADS // 2
    q_spec = pl.BlockSpec((None, tq, LANES), lambda bi, p, i: (bi, i, p))
    seq_spec = pl.BlockSpec((None, l, LANES), lambda bi, p, i: (bi, 0, p))
    ctx_spec = pl.BlockSpec((None, n_ctx, LANES), lambda bi, p, i: (bi, 0, p))
    bias_spec = pl.BlockSpec((2,) + bias.shape[1:], lambda bi, p, i: (p, 0, 0, 0))
    kernel = functools.partial(_na_kernel, rows_per_step=rows_per_step, group_size=min(group_size, rows_per_step),
                               grid_rows=grid_rows)
    return pl.pallas_call(
        kernel,
        grid=(b, pairs, grid_rows // rows_per_step),
        in_specs=[q_spec, seq_spec, seq_spec, ctx_spec, ctx_spec, bias_spec],
        out_specs=q_spec,
        out_shape=jax.ShapeDtypeStruct(q.shape, BF16),
        compiler_params=_params("parallel", "parallel", "arbitrary"),
        name="na_attention",
    )(q, k, v, k_ctx, v_ctx, bias)


def _out_ffn_kernel(x_ref, na_ref, gla_ref, mod_ref, gains_ref, wo_ref, wgu_ref, wd_ref, o_ref, *,
                    hidden_chunks, sub_tiles):
    gate_mix, shift_ffn, scale_ffn, gate_ffn = (mod_ref[i:i + 1, :] for i in (2, 3, 4, 5))
    g_post_mix, g_pre_ffn, g_post_ffn = (gains_ref[i:i + 1, :] for i in (0, 1, 2))
    half = na_ref.shape[-1]
    hidden = wd_ref.shape[0]
    ts = x_ref.shape[0] // sub_tiles
    rows = [slice(s * ts, (s + 1) * ts) for s in range(sub_tiles)]
    x1, h, acc = [None] * sub_tiles, [None] * sub_tiles, [None] * sub_tiles

    y = [_dot(na_ref[r, :], wo_ref[:half, :]) + _dot(gla_ref[r, :], wo_ref[half:, :]) for r in rows]

    def mix(s):
        x1[s] = x_ref[rows[s], :] + gate_mix * (_rms(y[s]) * g_post_mix)
        h[s] = ((_rms(x1[s]) * g_pre_ffn) * (1.0 + scale_ffn) + shift_ffn).astype(BF16)

    def ffn_chunk(s, c0, width):
        g = _dot(h[s], wgu_ref[:, c0:c0 + width])
        u = _dot(h[s], wgu_ref[:, hidden + c0:hidden + c0 + width])
        a = ((g * _sigmoid(g)) * u).astype(BF16)
        part = _dot(a, wd_ref[c0:c0 + width, :])
        acc[s] = part if acc[s] is None else acc[s] + part

    def finish(s):
        o_ref[rows[s], :] = x1[s] + gate_ffn * (_rms(acc[s]) * g_post_ffn)

    mix(0)
    for s in range(sub_tiles):
        for ci, (c0, width) in enumerate(hidden_chunks):
            ffn_chunk(s, c0, width)
            if ci == 0 and s + 1 < sub_tiles:
                mix(s + 1)
            if ci == 0 and s > 0:
                finish(s - 1)
    finish(sub_tiles - 1)


def _out_ffn(x, na_out, gla_out, mods, gains, w_out, w_gate_up, w_down, tm, sub_tiles):
    b, l, d = x.shape
    hidden = w_down.shape[0]
    chunk = 1024
    hidden_chunks = tuple((c0, min(chunk, hidden - c0)) for c0 in range(0, hidden, chunk))
    tok = lambda width: pl.BlockSpec((None, tm, width), lambda bi, i: (bi, i, 0))
    kernel = functools.partial(_out_ffn_kernel, hidden_chunks=hidden_chunks, sub_tiles=sub_tiles)
    return pl.pallas_call(
        kernel,
        grid=(b, l // tm),
        in_specs=[tok(d), tok(na_out.shape[-1]), tok(gla_out.shape[-1]),
                  pl.BlockSpec((None,) + mods.shape[1:], lambda bi, i: (bi, 0, 0)),
                  _resident(gains.shape), _resident(w_out.shape), _resident(w_gate_up.shape),
                  _resident(w_down.shape)],
        out_specs=tok(d),
        out_shape=jax.ShapeDtypeStruct(x.shape, x.dtype),
        compiler_params=_params("parallel", "parallel"),
        name="out_ffn",
    )(x, na_out, gla_out, mods, gains, w_out, w_gate_up, w_down)


def kernel(x, c, ctx, c_ctx, w_mod, b_mod, norm_pre_mix, norm_post_mix, norm_pre_ffn, norm_post_ffn, w_in, na_rpb, gla_wa2_f, gla_ba_f, gla_wa2_b, gla_ba_b, gla_norm, w_out, w_gate_up, w_down):
    assert w_mod.shape[0] == 1, "single-layer stack: the context stream is only read, never updated"
    b, l, d = x.shape
    n_ctx = ctx.shape[1]
    rank = GLA_GATE_RANK

    cond = jnp.concatenate([c, c_ctx[None, :]], axis=0)
    cond = jnp.pad(cond, ((0, -cond.shape[0] % 8), (0, 0)))
    mods = _adaln_mod(cond, w_mod[0], b_mod[0][None, :])
    lat_mods = mods[:b].reshape(b, 6, d)
    ctx_mods = mods[b:b + 1].reshape(1, 6, d)

    main_width = w_in.shape[-1] - 2 * rank
    w_all = jnp.concatenate([w_in[0, :, :main_width].astype(BF16),
                             jnp.pad(w_in[0, :, main_width:].astype(BF16), ((0, 0), (0, LANES - 2 * rank)))], axis=1)
    w_a2 = jnp.zeros((LANES, 2 * GLA_QK_WIDTH), F32)
    w_a2 = w_a2.at[:rank, :GLA_QK_WIDTH].set(gla_wa2_f[0]).at[rank:2 * rank, GLA_QK_WIDTH:].set(gla_wa2_b[0])
    w_a2 = w_a2.astype(BF16)
    b_a2 = jnp.concatenate([gla_ba_f[0], gla_ba_b[0]])[None, :]

    proj = functools.partial(_in_proj, gain=norm_pre_mix, w_all=w_all, w_a2=w_a2, b_a2=b_a2)
    naq, nak, nav, gq, gk, gv, gr, lf, lb = proj(x, lat_mods, rotate=True, tm=min(PROJ_TILE, l),
                                                 sub_tiles=PROJ_SUB_TILES)
    ctx_streams = proj(ctx.reshape(1, b * n_ctx, d), ctx_mods, rotate=False, tm=min(CTX_PROJ_TILE, b * n_ctx),
                       sub_tiles=1)
    _, cnk, cnv, _, cgk, cgv, _, clf, clb = [s.reshape(b, n_ctx, s.shape[-1]) for s in ctx_streams]

    zero_state = jnp.zeros((b, GLA_HEADS, GLA_VAL_DIM, LANES), F32)
    scan = functools.partial(_gla_scan, group_size=GLA_GROUP)
    _, s_fwd = scan(cgk, cgk, clf, cgv, zero_state, reverse=False, tb=min(GLA_BLOCK, n_ctx))
    _, s_bwd = scan(cgk, cgk, clb, cgv, zero_state, reverse=True, tb=min(GLA_BLOCK, n_ctx))
    o_fwd, _ = scan(gq, gk, lf, gv, s_fwd, reverse=False, tb=min(GLA_BLOCK, l))
    gla_out, _ = scan(gq, gk, lb, gv, s_bwd, reverse=True, tb=min(GLA_BLOCK, l), prev=o_fwd, gate=gr, gain=gla_norm)

    na_out = _na_attention(naq, nak, nav, cnk, cnv, _na_bias_table(na_rpb[0]),
                           rows_per_step=min(NA_ROWS_PER_STEP, l // GRID_W), group_size=NA_GROUP)

    gains = jnp.concatenate([norm_post_mix, norm_pre_ffn, norm_post_ffn], axis=0)
    return _out_ffn(x, na_out, gla_out, lat_mods, gains, w_out[0].astype(BF16), w_gate_up[0].astype(BF16),
                    w_down[0].astype(BF16), tm=min(FFN_TILE, l), sub_tiles=FFN_SUB_TILES)
```

```python
import functools

import numpy as np
import jax
import jax.numpy as jnp
from jax import lax
from jax.experimental import pallas as pl
from jax.experimental.pallas import tpu as pltpu

F32 = jnp.float32
BF16 = jnp.bfloat16

GRID_W = 64
NA_HEADS = 8
NA_HEAD_DIM = 64
NA_WIN_ROWS = 8
NA_WIN_COLS = 16
GLA_HEADS = 4
GLA_KEY_DIM = 64
GLA_VAL_DIM = 128
GLA_GATE_RANK = 16
GLA_GATE_TAU = 16.0
GLA_CHUNK = 64
ROPE_BASE = 10000.0
NORM_EPS = 1e-6

NA_WIDTH = NA_HEADS * NA_HEAD_DIM
GLA_QK_WIDTH = GLA_HEADS * GLA_KEY_DIM
GLA_V_WIDTH = GLA_HEADS * GLA_VAL_DIM

COL_NA_Q, COL_NA_K, COL_NA_V = 0, NA_WIDTH, 2 * NA_WIDTH
COL_GLA_Q = 3 * NA_WIDTH
COL_GLA_K = COL_GLA_Q + GLA_QK_WIDTH
COL_GLA_V = COL_GLA_K + GLA_QK_WIDTH
COL_GR = COL_GLA_V + GLA_V_WIDTH
PROJ_WIDTH = COL_GR + GLA_V_WIDTH

LANES = 128
MASK_VALUE = -1e30
LOG2_E = 1.4426950408889634
NA_Q_SCALE = NA_HEAD_DIM ** -0.5 * LOG2_E
VMEM_LIMIT_BYTES = 56 * 1024 * 1024

PROJ_TILE, PROJ_SUB_TILES = 1024, 4
CTX_PROJ_TILE = 512
FFN_TILE, FFN_SUB_TILES = 1024, 4
GLA_BLOCK, GLA_GROUP = 4096, 2
NA_ROWS_PER_STEP, NA_GROUP = 128, 1

NT_DIMS = (((1,), (1,)), ((), ()))
TN_DIMS = (((0,), (0,)), ((), ()))


def _dot(a, b):
    return jnp.dot(a, b, preferred_element_type=F32)


def _dot_nt(a, b):
    return lax.dot_general(a, b, NT_DIMS, preferred_element_type=F32)


def _dot_tn(a, b):
    return lax.dot_general(a, b, TN_DIMS, preferred_element_type=F32)


def _split_bf16(a):
    hi = a.astype(BF16)
    lo = (a - hi.astype(F32)).astype(BF16)
    return hi, lo


def _sigmoid(x):
    return 1.0 / (1.0 + jnp.exp(-x))


def _rms(x):
    return x * lax.rsqrt(jnp.mean(x * x, axis=-1, keepdims=True) + NORM_EPS)


def _fold_slabs(op, *arrays):
    slabs = [a[:, j:j + LANES] for a in arrays for j in range(0, a.shape[1], LANES)]
    return functools.reduce(op, slabs)


def _params(*semantics):
    return pltpu.CompilerParams(dimension_semantics=semantics, vmem_limit_bytes=VMEM_LIMIT_BYTES)


def _resident(shape):
    zeros = (0,) * len(shape)
    return pl.BlockSpec(shape, lambda *_: zeros)


def _mod_kernel(c_ref, w_ref, b_ref, o_ref):
    c = c_ref[...]
    s_hi, s_lo = _split_bf16(c * _sigmoid(c))
    w_hi, w_lo = _split_bf16(w_ref[...])
    o_ref[...] = _dot(s_hi, w_hi) + (_dot(s_hi, w_lo) + _dot(s_lo, w_hi)) + b_ref[...]


def _adaln_mod(cond, w_mod, b_mod):
    r, d = cond.shape
    n = w_mod.shape[1]
    tn = 1024
    return pl.pallas_call(
        _mod_kernel,
        grid=(n // tn,),
        in_specs=[pl.BlockSpec((r, d), lambda j: (0, 0)),
                  pl.BlockSpec((d, tn), lambda j: (0, j)),
                  pl.BlockSpec((1, tn), lambda j: (0, j))],
        out_specs=pl.BlockSpec((r, tn), lambda j: (0, j)),
        out_shape=jax.ShapeDtypeStruct((r, n), F32),
        compiler_params=_params("arbitrary"),
        name="adaln_mod",
    )(cond, w_mod, b_mod)


def _rope_tables(seq_len, rotate):
    if not rotate:
        return np.ones((seq_len, LANES), np.float32), np.zeros((seq_len, LANES), np.float32)
    quarter = GLA_KEY_DIM // 4
    t = np.arange(seq_len)
    inv = ROPE_BASE ** (-np.arange(quarter, dtype=np.float64) / quarter)
    ang_row = (t // GRID_W)[:, None] * inv[None, :]
    ang_col = (t % GRID_W)[:, None] * inv[None, :]
    cos = np.concatenate([np.cos(ang_row)] * 2 + [np.cos(ang_col)] * 2, axis=-1)
    sin = np.concatenate([-np.sin(ang_row), np.sin(ang_row), -np.sin(ang_col), np.sin(ang_col)], axis=-1)
    reps = LANES // GLA_KEY_DIM
    return np.tile(cos, (1, reps)).astype(np.float32), np.tile(sin, (1, reps)).astype(np.float32)


def _in_proj_kernel(x_ref, mod_ref, g_ref, w_ref, wa_ref, ba_ref, cos_ref, sin_ref, proj_ref, gate_ref, *, sub_tiles):
    shift, scale = mod_ref[0:1, :], mod_ref[1:2, :]
    ts = x_ref.shape[0] // sub_tiles
    quarter = GLA_KEY_DIM // 4
    lane = lax.broadcasted_iota(jnp.int32, (ts, LANES), 1)
    first_half = (lane & quarter) == 0
    hb = [None] * sub_tiles

    def normalize(s):
        x = x_ref[s * ts:(s + 1) * ts, :]
        hb[s] = ((_rms(x) * g_ref[...]) * (1.0 + scale) + shift).astype(BF16)

    def project(s, after_first_dot):
        rows = slice(s * ts, (s + 1) * ts)

        def proj(c0, width):
            return _dot(hb[s], w_ref[:, c0:c0 + width])

        def store(c0, value):
            proj_ref[rows, c0:c0 + value.shape[1]] = value.astype(BF16)

        gr_gate = proj(COL_GR, GLA_V_WIDTH + LANES)
        after_first_dot()
        store(COL_GR, gr_gate[:, :GLA_V_WIDTH])
        low_rank = gr_gate[:, GLA_V_WIDTH:].astype(BF16)
        z = _dot(low_rank, wa_ref[...]) + ba_ref[...]
        gate_ref[rows, :] = (jnp.minimum(z, 0.0) - jnp.log(1.0 + jnp.exp(-jnp.abs(z)))) * (1.0 / GLA_GATE_TAU)

        store(COL_NA_Q, proj(COL_NA_Q, NA_WIDTH) * NA_Q_SCALE)
        store(COL_NA_K, proj(COL_NA_K, NA_WIDTH))
        store(COL_NA_V, proj(COL_NA_V, NA_WIDTH))

        cos, sin = cos_ref[rows, :], sin_ref[rows, :]

        def rope(t):
            partner = jnp.where(first_half, pltpu.roll(t, LANES - quarter, 1), pltpu.roll(t, quarter, 1))
            return t * cos + partner * sin

        gqk = proj(COL_GLA_Q, 2 * GLA_QK_WIDTH)
        for j in range(GLA_QK_WIDTH // LANES):
            q_lanes = slice(j * LANES, (j + 1) * LANES)
            k_lanes = slice(GLA_QK_WIDTH + j * LANES, GLA_QK_WIDTH + (j + 1) * LANES)
            store(COL_GLA_Q + j * LANES, rope(gqk[:, q_lanes]) * GLA_KEY_DIM ** -0.5)
            store(COL_GLA_K + j * LANES, rope(gqk[:, k_lanes]))

        store(COL_GLA_V, proj(COL_GLA_V, GLA_V_WIDTH))

    normalize(0)
    for s in range(sub_tiles):
        project(s, (lambda nxt=s + 1: normalize(nxt)) if s + 1 < sub_tiles else (lambda: None))


def _in_proj(x, mods, gain, w_all, w_a2, b_a2, rotate, tm, sub_tiles):
    b, l, d = x.shape
    per_batch = mods.shape[0] > 1
    cos, sin = _rope_tables(l, rotate)
    tok = lambda width: pl.BlockSpec((None, tm, width), lambda bi, i: (bi, i, 0))
    out_widths = [PROJ_WIDTH, 2 * GLA_QK_WIDTH]
    out_dtypes = [BF16, F32]
    return pl.pallas_call(
        functools.partial(_in_proj_kernel, sub_tiles=sub_tiles),
        grid=(b, l // tm),
        in_specs=[tok(d),
                  pl.BlockSpec((None,) + mods.shape[1:], lambda bi, i: (bi if per_batch else 0, 0, 0)),
                  _resident(gain.shape), _resident(w_all.shape),
                  _resident(w_a2.shape), _resident(b_a2.shape),
                  pl.BlockSpec((tm, LANES), lambda bi, i: (i, 0)),
                  pl.BlockSpec((tm, LANES), lambda bi, i: (i, 0))],
        out_specs=[tok(w) for w in out_widths],
        out_shape=[jax.ShapeDtypeStruct((b, l, w), dt) for w, dt in zip(out_widths, out_dtypes)],
        compiler_params=_params("parallel", "parallel"),
        name="in_proj",
    )(x, mods, gain, w_all, w_a2, b_a2, cos, sin)


def _gla_kernel(*refs, reverse, finalize, chunks, group_size):
    if finalize:
        q_ref, k_ref, lg_ref, v_ref, s0_ref, prev_ref, gr_ref, gain_ref, out_ref, sfin_ref, st_ref = refs
    else:
        q_ref, k_ref, lg_ref, v_ref, s0_ref, out_ref, sfin_ref, st_ref = refs
    step = pl.program_id(2)

    @pl.when(step == 0)
    def _():
        st_ref[...] = s0_ref[...]

    c = GLA_CHUNK
    row = lax.broadcasted_iota(jnp.int32, (c, c), 0)
    col = lax.broadcasted_iota(jnp.int32, (c, c), 1)
    tri = (row <= col) if reverse else (row >= col)
    tri_b = tri.astype(BF16)
    lane = lax.broadcasted_iota(jnp.int32, (c, LANES), 1)
    head_lanes = [lane < GLA_KEY_DIM, lane >= GLA_KEY_DIM]
    end_row = 0 if reverse else c - 1

    heads = range(2)
    rows = [slice(i * c, (i + 1) * c) for i in range(chunks)]
    vals = [slice(h * GLA_VAL_DIM, (h + 1) * GLA_VAL_DIM) for h in heads]
    per_chunk = lambda: [None] * chunks
    bcum, q_heads, k_inv, k_end, decay = per_chunk(), per_chunk(), per_chunk(), per_chunk(), per_chunk()
    v_t, scores, kv, attn, state_before = per_chunk(), per_chunk(), per_chunk(), per_chunk(), per_chunk()
    state = [st_ref[h] for h in heads]

    def log_decay(group):
        halves = [_split_bf16(lg_ref[rows[i], :]) for i in group]
        hi = jnp.concatenate([h_[0] for h_ in halves], axis=1)
        lo = jnp.concatenate([h_[1] for h_ in halves], axis=1)
        both = _dot(tri_b, hi) + _dot(tri_b, lo)
        for n, i in enumerate(group):
            bcum[i] = both[:, n * LANES:(n + 1) * LANES]

    def decayed_operands(group):
        for i in group:
            b_end = bcum[i][end_row:end_row + 1, :]
            q_dec = q_ref[rows[i], :].astype(F32) * jnp.exp(bcum[i])
            k = k_ref[rows[i], :].astype(F32)
            q_heads[i] = [jnp.where(head_lanes[h], q_dec, 0.0).astype(BF16) for h in heads]
            k_inv[i] = (k * jnp.exp(-bcum[i])).astype(BF16)
            k_end[i] = (k * jnp.exp(b_end - bcum[i])).astype(BF16)
            decay[i] = jnp.exp(b_end)

    def chunk_products(group):
        for i in group:
            v_t[i] = [v_ref[rows[i], vals[h]].T for h in heads]
            scores[i] = [_dot_nt(q_heads[i][h], k_inv[i]) for h in heads]
            kv[i] = [_dot(v_t[i][h], k_end[i]) for h in heads]

    def mask_and_carry(group):
        for i in group:
            attn[i] = [jnp.where(tri, scores[i][h], 0.0).astype(BF16) for h in heads]
            state_before[i] = [state[h].astype(BF16) for h in heads]
            for h in heads:
                state[h] = state[h] * decay[i] + kv[i][h]

    def outputs(group):
        for i in group:
            for h in heads:
                lhs = jnp.concatenate([q_heads[i][h], attn[i][h]], axis=1)
                rhs = jnp.concatenate([state_before[i][h], v_t[i][h]], axis=1)
                o = _dot_nt(lhs, rhs)
                if finalize:
                    o = _rms(o + prev_ref[rows[i], vals[h]])
                    gate = gr_ref[rows[i], vals[h]].astype(F32)
                    out_ref[rows[i], vals[h]] = (o * gain_ref[:, vals[h]] * (gate * _sigmoid(gate))).astype(out_ref.dtype)
                else:
                    out_ref[rows[i], vals[h]] = o

    order = list(reversed(range(chunks))) if reverse else list(range(chunks))
    groups = [order[g:g + group_size] for g in range(0, chunks, group_size)]
    stages = [log_decay, decayed_operands, chunk_products, mask_and_carry, outputs]
    for tick in range(len(groups) + len(stages) - 1):
        for k, stage in enumerate(stages):
            if 0 <= tick - k < len(groups):
                stage(groups[tick - k])
    for h in heads:
        st_ref[h] = state[h]

    @pl.when(step == pl.num_programs(2) - 1)
    def _():
        sfin_ref[...] = st_ref[...]


def _gla_scan(proj, gates, s0, *, reverse, tb, group_size, prev=None, gain=None):
    b, l, _ = proj.shape
    finalize = prev is not None
    nb = l // tb
    pairs = GLA_HEADS // 2
    pair_vals = 2 * GLA_VAL_DIM
    blk = (lambda i: nb - 1 - i) if reverse else (lambda i: i)
    lanes_at = lambda col: pl.BlockSpec((None, tb, LANES), lambda bi, p, i: (bi, blk(i), col // LANES + p))
    vals_at = lambda col: pl.BlockSpec((None, tb, pair_vals), lambda bi, p, i: (bi, blk(i), col // pair_vals + p))
    st_spec = pl.BlockSpec((None, 2, GLA_VAL_DIM, LANES), lambda bi, p, i: (bi, p, 0, 0))
    v_spec = vals_at(0)
    in_specs = [lanes_at(COL_GLA_Q), lanes_at(COL_GLA_K), lanes_at(GLA_QK_WIDTH if reverse else 0),
                vals_at(COL_GLA_V), st_spec]
    args = [proj, proj, gates, proj, s0]
    if finalize:
        in_specs += [v_spec, vals_at(COL_GR), pl.BlockSpec((1, pair_vals), lambda bi, p, i: (0, p))]
        args += [prev, proj, gain]
    chunks = tb // GLA_CHUNK
    kernel = functools.partial(_gla_kernel, reverse=reverse, finalize=finalize, chunks=chunks,
                               group_size=min(group_size, chunks))
    return pl.pallas_call(
        kernel,
        grid=(b, pairs, nb),
        in_specs=in_specs,
        out_specs=[v_spec, st_spec],
        out_shape=[jax.ShapeDtypeStruct((b, l, GLA_V_WIDTH), BF16 if finalize else F32),
                   jax.ShapeDtypeStruct(s0.shape, F32)],
        scratch_shapes=[pltpu.VMEM((2, GLA_VAL_DIM, LANES), F32)],
        compiler_params=_params("parallel", "parallel", "arbitrary"),
        name="gla_scan_bwd" if reverse else "gla_scan_fwd",
    )(*args)


def _bias_kernel(rpb_ref, o_ref):
    kw, w = NA_WIN_COLS, GRID_W
    c = lax.broadcasted_iota(jnp.int32, (w, LANES), 0)
    lane = lax.broadcasted_iota(jnp.int32, (w, LANES), 1)
    kc = lane & (w - 1)
    start = jnp.clip(c - kw // 2, 0, w - kw)
    in_window = (kc >= start) & (kc < start + kw)
    second_row = lane >= w
    toeplitz = [pltpu.roll(jnp.broadcast_to(rpb_ref[r:r + 1, :], (w, LANES)), LANES - (w - 1), 1, stride=1, stride_axis=0)
                for r in range(rpb_ref.shape[0])]
    for e in range(o_ref.shape[0]):
        slab = jnp.where(second_row, pltpu.roll(toeplitz[e + 1], w, 1), toeplitz[e])
        o_ref[e] = jnp.where(in_window, slab * LOG2_E, MASK_VALUE)


def _na_bias_table(rpb):
    heads, n_rows, n_cols = rpb.shape
    lead = GRID_W - NA_WIN_COLS
    padded = jnp.pad(rpb, ((0, 0), (0, 0), (lead, LANES - lead - n_cols)))
    return pl.pallas_call(
        _bias_kernel,
        grid=(heads,),
        in_specs=[pl.BlockSpec((None, n_rows, LANES), lambda h: (h, 0, 0))],
        out_specs=pl.BlockSpec((None, n_rows - 1, GRID_W, LANES), lambda h: (h, 0, 0, 0)),
        out_shape=jax.ShapeDtypeStruct((heads, n_rows - 1, GRID_W, LANES), F32),
        compiler_params=_params("parallel"),
        name="na_bias_table",
    )(padded)


def _na_kernel(q_ref, k_ref, v_ref, kc_ref, vc_ref, bias_ref, o_ref, *, rows_per_step, group_size, grid_rows):
    kr, w = NA_WIN_ROWS, GRID_W
    band = kr * w
    lane = lax.broadcasted_iota(jnp.int32, (w, LANES), 1)
    first_head = lane < NA_HEAD_DIM
    base_row = pl.program_id(2) * rows_per_step
    per_row = lambda: [None] * rows_per_step
    band_rows, s_loc, s_ctx, p_loc, p_ctx, denom = per_row(), per_row(), per_row(), per_row(), per_row(), per_row()
    pair = [slice(n * 2 * w, (n + 1) * 2 * w) for n in range(group_size)]

    def logits(group):
        stacked = []
        for i in group:
            q = q_ref[i * w:(i + 1) * w, :]
            zero = jnp.zeros_like(q)
            stacked += [jnp.where(first_head, q, zero), jnp.where(first_head, zero, q)]
        q2 = jnp.concatenate(stacked, axis=0)
        s_ctx_group = _dot_nt(q2, kc_ref[...])
        for n, i in enumerate(group):
            r = base_row + i
            start = jnp.clip(r - kr // 2, 0, grid_rows - kr)
            band_rows[i] = pl.ds(pl.multiple_of(start * w, w), band)
            first_slab = kr - 1 - (r - start)
            bias = jnp.concatenate([jnp.concatenate([bias_ref[h, first_slab + 2 * j] for j in range(kr // 2)], axis=1)
                                    for h in range(2)], axis=0)
            s_loc[i] = _dot_nt(q2[pair[n], :], k_ref[band_rows[i], :]) + bias
            s_ctx[i] = s_ctx_group[pair[n], :]

    def softmax(group):
        for i in group:
            m = jnp.max(_fold_slabs(jnp.maximum, s_loc[i], s_ctx[i]), axis=-1, keepdims=True)
            p = jnp.exp2(s_loc[i] - m)
            p_c = jnp.exp2(s_ctx[i] - m)
            denom[i] = jnp.sum(_fold_slabs(jnp.add, p, p_c), axis=-1, keepdims=True)
            p_loc[i] = p.astype(BF16)
            p_ctx[i] = p_c.astype(BF16)

    def weighted_values(group):
        pv_ctx = _dot(jnp.concatenate([p_ctx[i] for i in group], axis=0), vc_ref[...])
        for n, i in enumerate(group):
            pv = (_dot(p_loc[i], v_ref[band_rows[i], :]) + pv_ctx[pair[n], :]) * (1.0 / denom[i])
            o_ref[i * w:(i + 1) * w, :] = jnp.where(first_head, pv[:w], pv[w:]).astype(o_ref.dtype)

    groups = [list(range(g, g + group_size)) for g in range(0, rows_per_step, group_size)]
    stages = [logits, softmax, weighted_values]
    for tick in range(len(groups) + len(stages) - 1):
        for k, stage in enumerate(stages):
            if 0 <= tick - k < len(groups):
                stage(groups[tick - k])


def _na_attention(proj, ctx_proj, bias, rows_per_step, group_size):
    b, l, _ = proj.shape
    n_ctx = ctx_proj.shape[1]
    grid_rows = l // GRID_W
    tq = rows_per_step * GRID_W
    pairs = NA_HEADS // 2
    q_spec = pl.BlockSpec((None, tq, LANES), lambda bi, p, i: (bi, i, COL_NA_Q // LANES + p))
    seq_at = lambda col: pl.BlockSpec((None, l, LANES), lambda bi, p, i: (bi, 0, col // LANES + p))
    ctx_at = lambda col: pl.BlockSpec((None, n_ctx, LANES), lambda bi, p, i: (bi, 0, col // LANES + p))
    out_spec = pl.BlockSpec((None, tq, LANES), lambda bi, p, i: (bi, i, p))
    bias_spec = pl.BlockSpec((2,) + bias.shape[1:], lambda bi, p, i: (p, 0, 0, 0))
    kernel = functools.partial(_na_kernel, rows_per_step=rows_per_step, group_size=min(group_size, rows_per_step),
                               grid_rows=grid_rows)
    return pl.pallas_call(
        kernel,
        grid=(b, pairs, grid_rows // rows_per_step),
        in_specs=[q_spec, seq_at(COL_NA_K), seq_at(COL_NA_V), ctx_at(COL_NA_K), ctx_at(COL_NA_V), bias_spec],
        out_specs=out_spec,
        out_shape=jax.ShapeDtypeStruct((b, l, NA_WIDTH), BF16),
        compiler_params=_params("parallel", "parallel", "arbitrary"),
        name="na_attention",
    )(proj, proj, proj, ctx_proj, ctx_proj, bias)


def _out_ffn_kernel(x_ref, na_ref, gla_ref, mod_ref, gains_ref, wo_ref, wgu_ref, wd_ref, o_ref, *,
                    hidden_chunks, sub_tiles):
    gate_mix, shift_ffn, scale_ffn, gate_ffn = (mod_ref[i:i + 1, :] for i in (2, 3, 4, 5))
    g_post_mix, g_pre_ffn, g_post_ffn = (gains_ref[i:i + 1, :] for i in (0, 1, 2))
    half = na_ref.shape[-1]
    hidden = wd_ref.shape[0]
    ts = x_ref.shape[0] // sub_tiles
    rows = [slice(s * ts, (s + 1) * ts) for s in range(sub_tiles)]
    x1, h, acc = [None] * sub_tiles, [None] * sub_tiles, [None] * sub_tiles

    y = [_dot(na_ref[r, :], wo_ref[:half, :]) + _dot(gla_ref[r, :], wo_ref[half:, :]) for r in rows]

    def mix(s):
        x1[s] = x_ref[rows[s], :] + gate_mix * (_rms(y[s]) * g_post_mix)
        h[s] = ((_rms(x1[s]) * g_pre_ffn) * (1.0 + scale_ffn) + shift_ffn).astype(BF16)

    def ffn_chunk(s, c0, width):
        g = _dot(h[s], wgu_ref[:, c0:c0 + width])
        u = _dot(h[s], wgu_ref[:, hidden + c0:hidden + c0 + width])
        a = ((g * _sigmoid(g)) * u).astype(BF16)
        part = _dot(a, wd_ref[c0:c0 + width, :])
        acc[s] = part if acc[s] is None else acc[s] + part

    def finish(s):
        o_ref[rows[s], :] = x1[s] + gate_ffn * (_rms(acc[s]) * g_post_ffn)

    mix(0)
    for s in range(sub_tiles):
        for ci, (c0, width) in enumerate(hidden_chunks):
            ffn_chunk(s, c0, width)
            if ci == 0 and s + 1 < sub_tiles:
                mix(s + 1)
            if ci == 0 and s > 0:
                finish(s - 1)
    finish(sub_tiles - 1)


def _out_ffn(x, na_out, gla_out, mods, gains, w_out, w_gate_up, w_down, tm, sub_tiles):
    b, l, d = x.shape
    hidden = w_down.shape[0]
    chunk = 1024
    hidden_chunks = tuple((c0, min(chunk, hidden - c0)) for c0 in range(0, hidden, chunk))
    tok = lambda width: pl.BlockSpec((None, tm, width), lambda bi, i: (bi, i, 0))
    kernel = functools.partial(_out_ffn_kernel, hidden_chunks=hidden_chunks, sub_tiles=sub_tiles)
    return pl.pallas_call(
        kernel,
        grid=(b, l // tm),
        in_specs=[tok(d), tok(na_out.shape[-1]), tok(gla_out.shape[-1]),
                  pl.BlockSpec((None,) + mods.shape[1:], lambda bi, i: (bi, 0, 0)),
                  _resident(gains.shape), _resident(w_out.shape), _resident(w_gate_up.shape),
                  _resident(w_down.shape)],
        out_specs=tok(d),
        out_shape=jax.ShapeDtypeStruct(x.shape, x.dtype),
        compiler_params=_params("parallel", "parallel"),
        name="out_ffn",
    )(x, na_out, gla_out, mods, gains, w_out, w_gate_up, w_down)


def kernel(x, c, ctx, c_ctx, w_mod, b_mod, norm_pre_mix, norm_post_mix, norm_pre_ffn, norm_post_ffn, w_in, na_rpb, gla_wa2_f, gla_ba_f, gla_wa2_b, gla_ba_b, gla_norm, w_out, w_gate_up, w_down):
    assert w_mod.shape[0] == 1, "single-layer stack: the context stream is only read, never updated"
    b, l, d = x.shape
    n_ctx = ctx.shape[1]
    rank = GLA_GATE_RANK

    cond = jnp.concatenate([c, c_ctx[None, :]], axis=0)
    cond = jnp.pad(cond, ((0, -cond.shape[0] % 8), (0, 0)))
    mods = _adaln_mod(cond, w_mod[0], b_mod[0][None, :])
    lat_mods = mods[:b].reshape(b, 6, d)
    ctx_mods = mods[b:b + 1].reshape(1, 6, d)

    main_width = w_in.shape[-1] - 2 * rank
    w_all = jnp.concatenate([w_in[0, :, :main_width].astype(BF16),
                             jnp.pad(w_in[0, :, main_width:].astype(BF16), ((0, 0), (0, LANES - 2 * rank)))], axis=1)
    w_a2 = jnp.zeros((LANES, 2 * GLA_QK_WIDTH), F32)
    w_a2 = w_a2.at[:rank, :GLA_QK_WIDTH].set(gla_wa2_f[0]).at[rank:2 * rank, GLA_QK_WIDTH:].set(gla_wa2_b[0])
    w_a2 = w_a2.astype(BF16)
    b_a2 = jnp.concatenate([gla_ba_f[0], gla_ba_b[0]])[None, :]

    proj = functools.partial(_in_proj, gain=norm_pre_mix, w_all=w_all, w_a2=w_a2, b_a2=b_a2)
    lat_proj, lat_gates = proj(x, lat_mods, rotate=True, tm=min(PROJ_TILE, l), sub_tiles=PROJ_SUB_TILES)
    ctx_proj, ctx_gates = [s.reshape(b, n_ctx, s.shape[-1]) for s in
                           proj(ctx.reshape(1, b * n_ctx, d), ctx_mods, rotate=False,
                                tm=min(CTX_PROJ_TILE, b * n_ctx), sub_tiles=1)]

    zero_state = jnp.zeros((b, GLA_HEADS, GLA_VAL_DIM, LANES), F32)
    scan = functools.partial(_gla_scan, group_size=GLA_GROUP)
    _, s_fwd = scan(ctx_proj, ctx_gates, zero_state, reverse=False, tb=min(GLA_BLOCK, n_ctx))
    _, s_bwd = scan(ctx_proj, ctx_gates, zero_state, reverse=True, tb=min(GLA_BLOCK, n_ctx))
    o_fwd, _ = scan(lat_proj, lat_gates, s_fwd, reverse=False, tb=min(GLA_BLOCK, l))
    gla_out, _ = scan(lat_proj, lat_gates, s_bwd, reverse=True, tb=min(GLA_BLOCK, l), prev=o_fwd, gain=gla_norm)

    na_out = _na_attention(lat_proj, ctx_proj, _na_bias_table(na_rpb[0]),
                           rows_per_step=min(NA_ROWS_PER_STEP, l // GRID_W), group_size=NA_GROUP)

    gains = jnp.concatenate([norm_post_mix, norm_pre_ffn, norm_post_ffn], axis=0)
    return _out_ffn(x, na_out, gla_out, lat_mods, gains, w_out[0].astype(BF16), w_gate_up[0].astype(BF16),
                    w_down[0].astype(BF16), tm=min(FFN_TILE, l), sub_tiles=FFN_SUB_TILES)
```

```python
import functools

import numpy as np
import jax
import jax.numpy as jnp
from jax import lax
from jax.experimental import pallas as pl
from jax.experimental.pallas import tpu as pltpu

F32 = jnp.float32
BF16 = jnp.bfloat16

GRID_W = 64
NA_HEADS = 8
NA_HEAD_DIM = 64
NA_WIN_ROWS = 8
NA_WIN_COLS = 16
GLA_HEADS = 4
GLA_KEY_DIM = 64
GLA_VAL_DIM = 128
GLA_GATE_RANK = 16
GLA_GATE_TAU = 16.0
GLA_CHUNK = 64
ROPE_BASE = 10000.0
NORM_EPS = 1e-6

NA_WIDTH = NA_HEADS * NA_HEAD_DIM
GLA_QK_WIDTH = GLA_HEADS * GLA_KEY_DIM
GLA_V_WIDTH = GLA_HEADS * GLA_VAL_DIM

COL_NA_Q, COL_NA_K, COL_NA_V = 0, NA_WIDTH, 2 * NA_WIDTH
COL_GLA_Q = 3 * NA_WIDTH
COL_GLA_K = COL_GLA_Q + GLA_QK_WIDTH
COL_GLA_V = COL_GLA_K + GLA_QK_WIDTH
COL_GR = COL_GLA_V + GLA_V_WIDTH
PROJ_WIDTH = COL_GR + GLA_V_WIDTH

LANES = 128
MASK_VALUE = -1e30
LOG2_E = 1.4426950408889634
NA_Q_SCALE = NA_HEAD_DIM ** -0.5 * LOG2_E
VMEM_LIMIT_BYTES = 56 * 1024 * 1024

PROJ_TILE, PROJ_SUB_TILES = 1024, 4
CTX_PROJ_TILE = 512
FFN_TILE, FFN_SUB_TILES = 1024, 4
GLA_BLOCK = 4096
GLA_GROUP = {False: 8, True: 2}
NA_ROWS_PER_STEP, NA_GROUP = 128, 1

NT_DIMS = (((1,), (1,)), ((), ()))
TN_DIMS = (((0,), (0,)), ((), ()))


def _dot(a, b):
    return jnp.dot(a, b, preferred_element_type=F32)


def _dot_nt(a, b):
    return lax.dot_general(a, b, NT_DIMS, preferred_element_type=F32)


def _dot_tn(a, b):
    return lax.dot_general(a, b, TN_DIMS, preferred_element_type=F32)


def _split_bf16(a):
    hi = a.astype(BF16)
    lo = (a - hi.astype(F32)).astype(BF16)
    return hi, lo


def _sigmoid(x):
    return 1.0 / (1.0 + jnp.exp(-x))


def _rms(x):
    return x * lax.rsqrt(jnp.mean(x * x, axis=-1, keepdims=True) + NORM_EPS)


def _fold_slabs(op, *arrays):
    slabs = [a[:, j:j + LANES] for a in arrays for j in range(0, a.shape[1], LANES)]
    return functools.reduce(op, slabs)


def _params(*semantics):
    return pltpu.CompilerParams(dimension_semantics=semantics, vmem_limit_bytes=VMEM_LIMIT_BYTES)


def _resident(shape):
    zeros = (0,) * len(shape)
    return pl.BlockSpec(shape, lambda *_: zeros)


def _mod_kernel(c_ref, w_ref, b_ref, o_ref):
    c = c_ref[...]
    s_hi, s_lo = _split_bf16(c * _sigmoid(c))
    w_hi, w_lo = _split_bf16(w_ref[...])
    o_ref[...] = _dot(s_hi, w_hi) + (_dot(s_hi, w_lo) + _dot(s_lo, w_hi)) + b_ref[...]


def _adaln_mod(cond, w_mod, b_mod):
    r, d = cond.shape
    n = w_mod.shape[1]
    tn = 1024
    return pl.pallas_call(
        _mod_kernel,
        grid=(n // tn,),
        in_specs=[pl.BlockSpec((r, d), lambda j: (0, 0)),
                  pl.BlockSpec((d, tn), lambda j: (0, j)),
                  pl.BlockSpec((1, tn), lambda j: (0, j))],
        out_specs=pl.BlockSpec((r, tn), lambda j: (0, j)),
        out_shape=jax.ShapeDtypeStruct((r, n), F32),
        compiler_params=_params("arbitrary"),
        name="adaln_mod",
    )(cond, w_mod, b_mod)


def _rope_tables(seq_len, rotate):
    if not rotate:
        return np.ones((seq_len, LANES), np.float32), np.zeros((seq_len, LANES), np.float32)
    quarter = GLA_KEY_DIM // 4
    t = np.arange(seq_len)
    inv = ROPE_BASE ** (-np.arange(quarter, dtype=np.float64) / quarter)
    ang_row = (t // GRID_W)[:, None] * inv[None, :]
    ang_col = (t % GRID_W)[:, None] * inv[None, :]
    cos = np.concatenate([np.cos(ang_row)] * 2 + [np.cos(ang_col)] * 2, axis=-1)
    sin = np.concatenate([-np.sin(ang_row), np.sin(ang_row), -np.sin(ang_col), np.sin(ang_col)], axis=-1)
    reps = LANES // GLA_KEY_DIM
    return np.tile(cos, (1, reps)).astype(np.float32), np.tile(sin, (1, reps)).astype(np.float32)


def _in_proj_kernel(x_ref, mod_ref, g_ref, w_ref, wa_ref, ba_ref, cos_ref, sin_ref, proj_ref, gate_ref, *, sub_tiles):
    shift, scale = mod_ref[0:1, :], mod_ref[1:2, :]
    ts = x_ref.shape[0] // sub_tiles
    quarter = GLA_KEY_DIM // 4
    lane = lax.broadcasted_iota(jnp.int32, (ts, LANES), 1)
    first_half = (lane & quarter) == 0
    hb = [None] * sub_tiles

    def normalize(s):
        x = x_ref[s * ts:(s + 1) * ts, :]
        hb[s] = ((_rms(x) * g_ref[...]) * (1.0 + scale) + shift).astype(BF16)

    def project(s, after_first_dot):
        rows = slice(s * ts, (s + 1) * ts)

        def proj(c0, width):
            return _dot(hb[s], w_ref[:, c0:c0 + width])

        def store(c0, value):
            proj_ref[rows, c0:c0 + value.shape[1]] = value.astype(BF16)

        gr_gate = proj(COL_GR, GLA_V_WIDTH + LANES)
        after_first_dot()
        store(COL_GR, gr_gate[:, :GLA_V_WIDTH])
        low_rank = gr_gate[:, GLA_V_WIDTH:].astype(BF16)
        z = _dot(low_rank, wa_ref[...]) + ba_ref[...]
        gate_ref[rows, :] = (jnp.minimum(z, 0.0) - jnp.log(1.0 + jnp.exp(-jnp.abs(z)))) * (1.0 / GLA_GATE_TAU)

        store(COL_NA_Q, proj(COL_NA_Q, NA_WIDTH) * NA_Q_SCALE)
        store(COL_NA_K, proj(COL_NA_K, NA_WIDTH))
        store(COL_NA_V, proj(COL_NA_V, NA_WIDTH))

        cos, sin = cos_ref[rows, :], sin_ref[rows, :]

        def rope(t):
            partner = jnp.where(first_half, pltpu.roll(t, LANES - quarter, 1), pltpu.roll(t, quarter, 1))
            return t * cos + partner * sin

        gqk = proj(COL_GLA_Q, 2 * GLA_QK_WIDTH)
        for j in range(GLA_QK_WIDTH // LANES):
            q_lanes = slice(j * LANES, (j + 1) * LANES)
            k_lanes = slice(GLA_QK_WIDTH + j * LANES, GLA_QK_WIDTH + (j + 1) * LANES)
            store(COL_GLA_Q + j * LANES, rope(gqk[:, q_lanes]) * GLA_KEY_DIM ** -0.5)
            store(COL_GLA_K + j * LANES, rope(gqk[:, k_lanes]))

        store(COL_GLA_V, proj(COL_GLA_V, GLA_V_WIDTH))

    normalize(0)
    for s in range(sub_tiles):
        project(s, (lambda nxt=s + 1: normalize(nxt)) if s + 1 < sub_tiles else (lambda: None))


def _in_proj(x, mods, gain, w_all, w_a2, b_a2, rotate, tm, sub_tiles):
    b, l, d = x.shape
    per_batch = mods.shape[0] > 1
    cos, sin = _rope_tables(l, rotate)
    tok = lambda width: pl.BlockSpec((None, tm, width), lambda bi, i: (bi, i, 0))
    out_widths = [PROJ_WIDTH, 2 * GLA_QK_WIDTH]
    out_dtypes = [BF16, F32]
    return pl.pallas_call(
        functools.partial(_in_proj_kernel, sub_tiles=sub_tiles),
        grid=(b, l // tm),
        in_specs=[tok(d),
                  pl.BlockSpec((None,) + mods.shape[1:], lambda bi, i: (bi if per_batch else 0, 0, 0)),
                  _resident(gain.shape), _resident(w_all.shape),
                  _resident(w_a2.shape), _resident(b_a2.shape),
                  pl.BlockSpec((tm, LANES), lambda bi, i: (i, 0)),
                  pl.BlockSpec((tm, LANES), lambda bi, i: (i, 0))],
        out_specs=[tok(w) for w in out_widths],
        out_shape=[jax.ShapeDtypeStruct((b, l, w), dt) for w, dt in zip(out_widths, out_dtypes)],
        compiler_params=_params("parallel", "parallel"),
        name="in_proj",
    )(x, mods, gain, w_all, w_a2, b_a2, cos, sin)


def _gla_kernel(*refs, reverse, finalize, chunks, group_size):
    if finalize:
        q_ref, k_ref, lg_ref, v_ref, s0_ref, prev_ref, gr_ref, gain_ref, out_ref, sfin_ref, st_ref = refs
    else:
        q_ref, k_ref, lg_ref, v_ref, s0_ref, out_ref, sfin_ref, st_ref = refs
    step = pl.program_id(2)

    @pl.when(step == 0)
    def _():
        st_ref[...] = s0_ref[...]

    c = GLA_CHUNK
    row = lax.broadcasted_iota(jnp.int32, (c, c), 0)
    col = lax.broadcasted_iota(jnp.int32, (c, c), 1)
    tri = (row <= col) if reverse else (row >= col)
    tri_b = tri.astype(BF16)
    row2 = lax.broadcasted_iota(jnp.int32, (2 * c, 2 * c), 0) & (c - 1)
    col2 = lax.broadcasted_iota(jnp.int32, (2 * c, 2 * c), 1) & (c - 1)
    tri_pair = (row2 <= col2) if reverse else (row2 >= col2)
    lane = lax.broadcasted_iota(jnp.int32, (c, LANES), 1)
    first_head = lane < GLA_KEY_DIM
    end_row = 0 if reverse else c - 1

    def stack_heads(t):
        return jnp.concatenate([jnp.where(first_head, t, 0.0), jnp.where(first_head, 0.0, t)], axis=0).astype(BF16)

    heads = range(2)
    rows = [slice(i * c, (i + 1) * c) for i in range(chunks)]
    vals = [slice(h * GLA_VAL_DIM, (h + 1) * GLA_VAL_DIM) for h in heads]
    per_chunk = lambda: [None] * chunks
    bcum, q2, k2_inv, k2_end, decay = per_chunk(), per_chunk(), per_chunk(), per_chunk(), per_chunk()
    v2_t, scores, kv, attn, state_before = per_chunk(), per_chunk(), per_chunk(), per_chunk(), per_chunk()
    state = [st_ref[...]]

    def log_decay(group):
        halves = [_split_bf16(lg_ref[rows[i], :]) for i in group]
        hi = jnp.concatenate([h_[0] for h_ in halves], axis=1)
        lo = jnp.concatenate([h_[1] for h_ in halves], axis=1)
        both = _dot(tri_b, hi) + _dot(tri_b, lo)
        for n, i in enumerate(group):
            bcum[i] = both[:, n * LANES:(n + 1) * LANES]

    def decayed_operands(group):
        for i in group:
            b_end = bcum[i][end_row:end_row + 1, :]
            k = k_ref[rows[i], :].astype(F32)
            q2[i] = stack_heads(q_ref[rows[i], :].astype(F32) * jnp.exp(bcum[i]))
            k2_inv[i] = stack_heads(k * jnp.exp(-bcum[i]))
            k2_end[i] = stack_heads(k * jnp.exp(b_end - bcum[i]))
            decay[i] = jnp.exp(b_end)

    def chunk_products(group):
        for i in group:
            v2 = jnp.concatenate([v_ref[rows[i], vals[h]] for h in heads], axis=0)
            v2_t[i] = v2.T
            scores[i] = _dot_nt(q2[i], k2_inv[i])
            kv[i] = _dot(v2_t[i], k2_end[i])

    def mask_and_carry(group):
        for i in group:
            attn[i] = jnp.where(tri_pair, scores[i], 0.0).astype(BF16)
            state_before[i] = state[0].astype(BF16)
            state[0] = state[0] * decay[i] + kv[i]

    def outputs(group):
        for i in group:
            lhs = jnp.concatenate([q2[i], attn[i]], axis=1)
            rhs = jnp.concatenate([state_before[i], v2_t[i]], axis=1)
            o2 = _dot_nt(lhs, rhs)
            for h in heads:
                o = o2[h * c:(h + 1) * c, :]
                if finalize:
                    o = _rms(o + prev_ref[rows[i], vals[h]])
                    gate = gr_ref[rows[i], vals[h]].astype(F32)
                    out_ref[rows[i], vals[h]] = (o * gain_ref[:, vals[h]] * (gate * _sigmoid(gate))).astype(out_ref.dtype)
                else:
                    out_ref[rows[i], vals[h]] = o

    order = list(reversed(range(chunks))) if reverse else list(range(chunks))
    groups = [order[g:g + group_size] for g in range(0, chunks, group_size)]
    stages = [log_decay, decayed_operands, chunk_products, mask_and_carry, outputs]
    for tick in range(len(groups) + len(stages) - 1):
        for k, stage in enumerate(stages):
            if 0 <= tick - k < len(groups):
                stage(groups[tick - k])
    st_ref[...] = state[0]

    @pl.when(step == pl.num_programs(2) - 1)
    def _():
        sfin_ref[...] = st_ref[...]


def _gla_scan(proj, gates, s0, *, reverse, tb, prev=None, gain=None):
    b, l, _ = proj.shape
    finalize = prev is not None
    nb = l // tb
    pairs = GLA_HEADS // 2
    pair_vals = 2 * GLA_VAL_DIM
    blk = (lambda i: nb - 1 - i) if reverse else (lambda i: i)
    lanes_at = lambda col: pl.BlockSpec((None, tb, LANES), lambda bi, p, i: (bi, blk(i), col // LANES + p))
    vals_at = lambda col: pl.BlockSpec((None, tb, pair_vals), lambda bi, p, i: (bi, blk(i), col // pair_vals + p))
    st_spec = pl.BlockSpec((None, None, GLA_VAL_DIM, LANES), lambda bi, p, i: (bi, p, 0, 0))
    v_spec = vals_at(0)
    in_specs = [lanes_at(COL_GLA_Q), lanes_at(COL_GLA_K), lanes_at(GLA_QK_WIDTH if reverse else 0),
                vals_at(COL_GLA_V), st_spec]
    args = [proj, proj, gates, proj, s0]
    if finalize:
        in_specs += [v_spec, vals_at(COL_GR), pl.BlockSpec((1, pair_vals), lambda bi, p, i: (0, p))]
        args += [prev, proj, gain]
    chunks = tb // GLA_CHUNK
    kernel = functools.partial(_gla_kernel, reverse=reverse, finalize=finalize, chunks=chunks,
                               group_size=min(GLA_GROUP[reverse], chunks))
    return pl.pallas_call(
        kernel,
        grid=(b, pairs, nb),
        in_specs=in_specs,
        out_specs=[v_spec, st_spec],
        out_shape=[jax.ShapeDtypeStruct((b, l, GLA_V_WIDTH), BF16 if finalize else F32),
                   jax.ShapeDtypeStruct(s0.shape, F32)],
        scratch_shapes=[pltpu.VMEM((GLA_VAL_DIM, LANES), F32)],
        compiler_params=_params("parallel", "parallel", "arbitrary"),
        name="gla_scan_bwd" if reverse else "gla_scan_fwd",
    )(*args)


def _bias_kernel(rpb_ref, o_ref):
    kw, w = NA_WIN_COLS, GRID_W
    c = lax.broadcasted_iota(jnp.int32, (w, LANES), 0)
    lane = lax.broadcasted_iota(jnp.int32, (w, LANES), 1)
    kc = lane & (w - 1)
    start = jnp.clip(c - kw // 2, 0, w - kw)
    in_window = (kc >= start) & (kc < start + kw)
    second_row = lane >= w
    toeplitz = [pltpu.roll(jnp.broadcast_to(rpb_ref[r:r + 1, :], (w, LANES)), LANES - (w - 1), 1, stride=1, stride_axis=0)
                for r in range(rpb_ref.shape[0])]
    for e in range(o_ref.shape[0]):
        slab = jnp.where(second_row, pltpu.roll(toeplitz[e + 1], w, 1), toeplitz[e])
        o_ref[e] = jnp.where(in_window, slab * LOG2_E, MASK_VALUE)


def _na_bias_table(rpb):
    heads, n_rows, n_cols = rpb.shape
    lead = GRID_W - NA_WIN_COLS
    padded = jnp.pad(rpb, ((0, 0), (0, 0), (lead, LANES - lead - n_cols)))
    return pl.pallas_call(
        _bias_kernel,
        grid=(heads,),
        in_specs=[pl.BlockSpec((None, n_rows, LANES), lambda h: (h, 0, 0))],
        out_specs=pl.BlockSpec((None, n_rows - 1, GRID_W, LANES), lambda h: (h, 0, 0, 0)),
        out_shape=jax.ShapeDtypeStruct((heads, n_rows - 1, GRID_W, LANES), F32),
        compiler_params=_params("parallel"),
        name="na_bias_table",
    )(padded)


def _na_kernel(q_ref, k_ref, v_ref, kc_ref, vc_ref, bias_ref, o_ref, *, rows_per_step, group_size, grid_rows):
    kr, w = NA_WIN_ROWS, GRID_W
    band = kr * w
    lane = lax.broadcasted_iota(jnp.int32, (w, LANES), 1)
    first_head = lane < NA_HEAD_DIM
    base_row = pl.program_id(2) * rows_per_step
    per_row = lambda: [None] * rows_per_step
    band_rows, s_loc, s_ctx, p_loc, p_ctx, denom = per_row(), per_row(), per_row(), per_row(), per_row(), per_row()
    pair = [slice(n * 2 * w, (n + 1) * 2 * w) for n in range(group_size)]

    def logits(group):
        stacked = []
        for i in group:
            q = q_ref[i * w:(i + 1) * w, :]
            zero = jnp.zeros_like(q)
            stacked += [jnp.where(first_head, q, zero), jnp.where(first_head, zero, q)]
        q2 = jnp.concatenate(stacked, axis=0)
        s_ctx_group = _dot_nt(q2, kc_ref[...])
        for n, i in enumerate(group):
            r = base_row + i
            start = jnp.clip(r - kr // 2, 0, grid_rows - kr)
            band_rows[i] = pl.ds(pl.multiple_of(start * w, w), band)
            first_slab = kr - 1 - (r - start)
            bias = jnp.concatenate([jnp.concatenate([bias_ref[h, first_slab + 2 * j] for j in range(kr // 2)], axis=1)
                                    for h in range(2)], axis=0)
            s_loc[i] = _dot_nt(q2[pair[n], :], k_ref[band_rows[i], :]) + bias
            s_ctx[i] = s_ctx_group[pair[n], :]

    def softmax(group):
        for i in group:
            m = jnp.max(_fold_slabs(jnp.maximum, s_loc[i], s_ctx[i]), axis=-1, keepdims=True)
            p = jnp.exp2(s_loc[i] - m)
            p_c = jnp.exp2(s_ctx[i] - m)
            denom[i] = jnp.sum(_fold_slabs(jnp.add, p, p_c), axis=-1, keepdims=True)
            p_loc[i] = p.astype(BF16)
            p_ctx[i] = p_c.astype(BF16)

    def weighted_values(group):
        pv_ctx = _dot(jnp.concatenate([p_ctx[i] for i in group], axis=0), vc_ref[...])
        for n, i in enumerate(group):
            pv = (_dot(p_loc[i], v_ref[band_rows[i], :]) + pv_ctx[pair[n], :]) * (1.0 / denom[i])
            o_ref[i * w:(i + 1) * w, :] = jnp.where(first_head, pv[:w], pv[w:]).astype(o_ref.dtype)

    groups = [list(range(g, g + group_size)) for g in range(0, rows_per_step, group_size)]
    stages = [logits, softmax, weighted_values]
    for tick in range(len(groups) + len(stages) - 1):
        for k, stage in enumerate(stages):
            if 0 <= tick - k < len(groups):
                stage(groups[tick - k])


def _na_attention(proj, ctx_proj, bias, rows_per_step, group_size):
    b, l, _ = proj.shape
    n_ctx = ctx_proj.shape[1]
    grid_rows = l // GRID_W
    tq = rows_per_step * GRID_W
    pairs = NA_HEADS // 2
    q_spec = pl.BlockSpec((None, tq, LANES), lambda bi, p, i: (bi, i, COL_NA_Q // LANES + p))
    seq_at = lambda col: pl.BlockSpec((None, l, LANES), lambda bi, p, i: (bi, 0, col // LANES + p))
    ctx_at = lambda col: pl.BlockSpec((None, n_ctx, LANES), lambda bi, p, i: (bi, 0, col // LANES + p))
    out_spec = pl.BlockSpec((None, tq, LANES), lambda bi, p, i: (bi, i, p))
    bias_spec = pl.BlockSpec((2,) + bias.shape[1:], lambda bi, p, i: (p, 0, 0, 0))
    kernel = functools.partial(_na_kernel, rows_per_step=rows_per_step, group_size=min(group_size, rows_per_step),
                               grid_rows=grid_rows)
    return pl.pallas_call(
        kernel,
        grid=(b, pairs, grid_rows // rows_per_step),
        in_specs=[q_spec, seq_at(COL_NA_K), seq_at(COL_NA_V), ctx_at(COL_NA_K), ctx_at(COL_NA_V), bias_spec],
        out_specs=out_spec,
        out_shape=jax.ShapeDtypeStruct((b, l, NA_WIDTH), BF16),
        compiler_params=_params("parallel", "parallel", "arbitrary"),
        name="na_attention",
    )(proj, proj, proj, ctx_proj, ctx_proj, bias)


def _out_ffn_kernel(x_ref, na_ref, gla_ref, mod_ref, gains_ref, wo_ref, wgu_ref, wd_ref, o_ref, *,
                    hidden_chunks, sub_tiles):
    gate_mix, shift_ffn, scale_ffn, gate_ffn = (mod_ref[i:i + 1, :] for i in (2, 3, 4, 5))
    g_post_mix, g_pre_ffn, g_post_ffn = (gains_ref[i:i + 1, :] for i in (0, 1, 2))
    half = na_ref.shape[-1]
    hidden = wd_ref.shape[0]
    ts = x_ref.shape[0] // sub_tiles
    rows = [slice(s * ts, (s + 1) * ts) for s in range(sub_tiles)]
    x1, h, acc = [None] * sub_tiles, [None] * sub_tiles, [None] * sub_tiles

    y = [_dot(na_ref[r, :], wo_ref[:half, :]) + _dot(gla_ref[r, :], wo_ref[half:, :]) for r in rows]

    def mix(s):
        x1[s] = x_ref[rows[s], :] + gate_mix * (_rms(y[s]) * g_post_mix)
        h[s] = ((_rms(x1[s]) * g_pre_ffn) * (1.0 + scale_ffn) + shift_ffn).astype(BF16)

    def ffn_chunk(s, c0, width):
        g = _dot(h[s], wgu_ref[:, c0:c0 + width])
        u = _dot(h[s], wgu_ref[:, hidden + c0:hidden + c0 + width])
        a = ((g * _sigmoid(g)) * u).astype(BF16)
        part = _dot(a, wd_ref[c0:c0 + width, :])
        acc[s] = part if acc[s] is None else acc[s] + part

    def finish(s):
        o_ref[rows[s], :] = x1[s] + gate_ffn * (_rms(acc[s]) * g_post_ffn)

    mix(0)
    for s in range(sub_tiles):
        for ci, (c0, width) in enumerate(hidden_chunks):
            ffn_chunk(s, c0, width)
            if ci == 0 and s + 1 < sub_tiles:
                mix(s + 1)
            if ci == 0 and s > 0:
                finish(s - 1)
    finish(sub_tiles - 1)


def _out_ffn(x, na_out, gla_out, mods, gains, w_out, w_gate_up, w_down, tm, sub_tiles):
    b, l, d = x.shape
    hidden = w_down.shape[0]
    chunk = 1024
    hidden_chunks = tuple((c0, min(chunk, hidden - c0)) for c0 in range(0, hidden, chunk))
    tok = lambda width: pl.BlockSpec((None, tm, width), lambda bi, i: (bi, i, 0))
    kernel = functools.partial(_out_ffn_kernel, hidden_chunks=hidden_chunks, sub_tiles=sub_tiles)
    return pl.pallas_call(
        kernel,
        grid=(b, l // tm),
        in_specs=[tok(d), tok(na_out.shape[-1]), tok(gla_out.shape[-1]),
                  pl.BlockSpec((None,) + mods.shape[1:], lambda bi, i: (bi, 0, 0)),
                  _resident(gains.shape), _resident(w_out.shape), _resident(w_gate_up.shape),
                  _resident(w_down.shape)],
        out_specs=tok(d),
        out_shape=jax.ShapeDtypeStruct(x.shape, x.dtype),
        compiler_params=_params("parallel", "parallel"),
        name="out_ffn",
    )(x, na_out, gla_out, mods, gains, w_out, w_gate_up, w_down)


def kernel(x, c, ctx, c_ctx, w_mod, b_mod, norm_pre_mix, norm_post_mix, norm_pre_ffn, norm_post_ffn, w_in, na_rpb, gla_wa2_f, gla_ba_f, gla_wa2_b, gla_ba_b, gla_norm, w_out, w_gate_up, w_down):
    assert w_mod.shape[0] == 1, "single-layer stack: the context stream is only read, never updated"
    b, l, d = x.shape
    n_ctx = ctx.shape[1]
    rank = GLA_GATE_RANK

    cond = jnp.concatenate([c, c_ctx[None, :]], axis=0)
    cond = jnp.pad(cond, ((0, -cond.shape[0] % 8), (0, 0)))
    mods = _adaln_mod(cond, w_mod[0], b_mod[0][None, :])
    lat_mods = mods[:b].reshape(b, 6, d)
    ctx_mods = mods[b:b + 1].reshape(1, 6, d)

    main_width = w_in.shape[-1] - 2 * rank
    w_all = jnp.concatenate([w_in[0, :, :main_width].astype(BF16),
                             jnp.pad(w_in[0, :, main_width:].astype(BF16), ((0, 0), (0, LANES - 2 * rank)))], axis=1)
    w_a2 = jnp.zeros((LANES, 2 * GLA_QK_WIDTH), F32)
    w_a2 = w_a2.at[:rank, :GLA_QK_WIDTH].set(gla_wa2_f[0]).at[rank:2 * rank, GLA_QK_WIDTH:].set(gla_wa2_b[0])
    w_a2 = w_a2.astype(BF16)
    b_a2 = jnp.concatenate([gla_ba_f[0], gla_ba_b[0]])[None, :]

    proj = functools.partial(_in_proj, gain=norm_pre_mix, w_all=w_all, w_a2=w_a2, b_a2=b_a2)
    lat_proj, lat_gates = proj(x, lat_mods, rotate=True, tm=min(PROJ_TILE, l), sub_tiles=PROJ_SUB_TILES)
    ctx_proj, ctx_gates = [s.reshape(b, n_ctx, s.shape[-1]) for s in
                           proj(ctx.reshape(1, b * n_ctx, d), ctx_mods, rotate=False,
                                tm=min(CTX_PROJ_TILE, b * n_ctx), sub_tiles=1)]

    zero_state = jnp.zeros((b, GLA_HEADS // 2, GLA_VAL_DIM, LANES), F32)
    _, s_fwd = _gla_scan(ctx_proj, ctx_gates, zero_state, reverse=False, tb=min(GLA_BLOCK, n_ctx))
    _, s_bwd = _gla_scan(ctx_proj, ctx_gates, zero_state, reverse=True, tb=min(GLA_BLOCK, n_ctx))
    o_fwd, _ = _gla_scan(lat_proj, lat_gates, s_fwd, reverse=False, tb=min(GLA_BLOCK, l))
    gla_out, _ = _gla_scan(lat_proj, lat_gates, s_bwd, reverse=True, tb=min(GLA_BLOCK, l), prev=o_fwd, gain=gla_norm)

    na_out = _na_attention(lat_proj, ctx_proj, _na_bias_table(na_rpb[0]),
                           rows_per_step=min(NA_ROWS_PER_STEP, l // GRID_W), group_size=NA_GROUP)

    gains = jnp.concatenate([norm_post_mix, norm_pre_ffn, norm_post_ffn], axis=0)
    return _out_ffn(x, na_out, gla_out, lat_mods, gains, w_out[0].astype(BF16), w_gate_up[0].astype(BF16),
                    w_down[0].astype(BF16), tm=min(FFN_TILE, l), sub_tiles=FFN_SUB_TILES)
```

```python
import functools

import numpy as np
import jax
import jax.numpy as jnp
from jax import lax
from jax.experimental import pallas as pl
from jax.experimental.pallas import tpu as pltpu

F32 = jnp.float32
BF16 = jnp.bfloat16

GRID_W = 64
NA_HEADS = 8
NA_HEAD_DIM = 64
NA_WIN_ROWS = 8
NA_WIN_COLS = 16
GLA_HEADS = 4
GLA_KEY_DIM = 64
GLA_VAL_DIM = 128
GLA_GATE_RANK = 16
GLA_GATE_TAU = 16.0
GLA_CHUNK = 64
ROPE_BASE = 10000.0
NORM_EPS = 1e-6

NA_WIDTH = NA_HEADS * NA_HEAD_DIM
GLA_QK_WIDTH = GLA_HEADS * GLA_KEY_DIM
GLA_V_WIDTH = GLA_HEADS * GLA_VAL_DIM

COL_NA_Q, COL_NA_K, COL_NA_V = 0, NA_WIDTH, 2 * NA_WIDTH
COL_GLA_Q = 3 * NA_WIDTH
COL_GLA_K = COL_GLA_Q + GLA_QK_WIDTH
COL_GLA_V = COL_GLA_K + GLA_QK_WIDTH
COL_GR = COL_GLA_V + GLA_V_WIDTH
PROJ_WIDTH = COL_GR + GLA_V_WIDTH

LANES = 128
MASK_VALUE = -1e30
LOG2_E = 1.4426950408889634
NA_Q_SCALE = NA_HEAD_DIM ** -0.5 * LOG2_E
VMEM_LIMIT_BYTES = 56 * 1024 * 1024

MOD_TILE = 2048
PROJ_TILE, PROJ_SUB_TILES = 1024, 4
CTX_PROJ_TILE = 512
FFN_TILE, FFN_SUB_TILES = 1024, 4
GLA_BLOCK = 4096
GLA_GROUP = {False: 8, True: 2}
NA_ROWS_PER_STEP, NA_GROUP = 128, 1

NT_DIMS = (((1,), (1,)), ((), ()))


def _dot(a, b):
    return jnp.dot(a, b, preferred_element_type=F32)


def _dot_nt(a, b):
    return lax.dot_general(a, b, NT_DIMS, preferred_element_type=F32)


def _split_bf16(a):
    hi = a.astype(BF16)
    lo = (a - hi.astype(F32)).astype(BF16)
    return hi, lo


def _sigmoid(x):
    return 1.0 / (1.0 + jnp.exp(-x))


def _rms(x):
    return x * lax.rsqrt(jnp.mean(x * x, axis=-1, keepdims=True) + NORM_EPS)


def _fold_slabs(op, *arrays):
    slabs = [a[:, j:j + LANES] for a in arrays for j in range(0, a.shape[1], LANES)]
    return functools.reduce(op, slabs)


def _params(*semantics):
    return pltpu.CompilerParams(dimension_semantics=semantics, vmem_limit_bytes=VMEM_LIMIT_BYTES)


def _resident(shape):
    zeros = (0,) * len(shape)
    return pl.BlockSpec(shape, lambda *_: zeros)


def _mod_kernel(c_ref, w_ref, b_ref, o_ref):
    c = c_ref[...]
    s_hi, s_lo = _split_bf16(c * _sigmoid(c))
    w_hi, w_lo = _split_bf16(w_ref[...])
    o_ref[...] = _dot(s_hi, w_hi) + (_dot(s_hi, w_lo) + _dot(s_lo, w_hi)) + b_ref[...]


def _adaln_mod(cond, w_mod, b_mod):
    r, d = cond.shape
    n = w_mod.shape[1]
    tn = MOD_TILE
    return pl.pallas_call(
        _mod_kernel,
        grid=(n // tn,),
        in_specs=[pl.BlockSpec((r, d), lambda j: (0, 0)),
                  pl.BlockSpec((d, tn), lambda j: (0, j)),
                  pl.BlockSpec((1, tn), lambda j: (0, j))],
        out_specs=pl.BlockSpec((r, tn), lambda j: (0, j)),
        out_shape=jax.ShapeDtypeStruct((r, n), F32),
        compiler_params=_params("arbitrary"),
        name="adaln_mod",
    )(cond, w_mod, b_mod)


def _rope_tables(seq_len, rotate):
    if not rotate:
        return np.ones((seq_len, LANES), np.float32), np.zeros((seq_len, LANES), np.float32)
    quarter = GLA_KEY_DIM // 4
    t = np.arange(seq_len)
    inv = ROPE_BASE ** (-np.arange(quarter, dtype=np.float64) / quarter)
    ang_row = (t // GRID_W)[:, None] * inv[None, :]
    ang_col = (t % GRID_W)[:, None] * inv[None, :]
    cos = np.concatenate([np.cos(ang_row)] * 2 + [np.cos(ang_col)] * 2, axis=-1)
    sin = np.concatenate([-np.sin(ang_row), np.sin(ang_row), -np.sin(ang_col), np.sin(ang_col)], axis=-1)
    reps = LANES // GLA_KEY_DIM
    return np.tile(cos, (1, reps)).astype(np.float32), np.tile(sin, (1, reps)).astype(np.float32)


def _in_proj_kernel(x_ref, mod_ref, g_ref, w_ref, wa_ref, ba_ref, cos_ref, sin_ref, proj_ref, gate_ref, *, sub_tiles):
    shift, scale = mod_ref[0:1, :], mod_ref[1:2, :]
    ts = x_ref.shape[0] // sub_tiles
    quarter = GLA_KEY_DIM // 4
    lane = lax.broadcasted_iota(jnp.int32, (ts, LANES), 1)
    first_half = (lane & quarter) == 0
    hb = [None] * sub_tiles

    def normalize(s):
        x = x_ref[s * ts:(s + 1) * ts, :]
        hb[s] = ((_rms(x) * g_ref[...]) * (1.0 + scale) + shift).astype(BF16)

    def project(s, after_first_dot):
        rows = slice(s * ts, (s + 1) * ts)

        def proj(c0, width):
            return _dot(hb[s], w_ref[:, c0:c0 + width])

        def store(c0, value):
            proj_ref[rows, c0:c0 + value.shape[1]] = value.astype(BF16)

        gr_gate = proj(COL_GR, GLA_V_WIDTH + LANES)
        after_first_dot()
        store(COL_GR, gr_gate[:, :GLA_V_WIDTH])
        low_rank = gr_gate[:, GLA_V_WIDTH:].astype(BF16)
        z = _dot(low_rank, wa_ref[...]) + ba_ref[...]
        gate_ref[rows, :] = (jnp.minimum(z, 0.0) - jnp.log(1.0 + jnp.exp(-jnp.abs(z)))) * (1.0 / GLA_GATE_TAU)

        store(COL_NA_Q, proj(COL_NA_Q, NA_WIDTH) * NA_Q_SCALE)
        store(COL_NA_K, proj(COL_NA_K, NA_WIDTH))
        store(COL_NA_V, proj(COL_NA_V, NA_WIDTH))

        cos, sin = cos_ref[rows, :], sin_ref[rows, :]

        def rope(t):
            partner = jnp.where(first_half, pltpu.roll(t, LANES - quarter, 1), pltpu.roll(t, quarter, 1))
            return t * cos + partner * sin

        gqk = proj(COL_GLA_Q, 2 * GLA_QK_WIDTH)
        for j in range(GLA_QK_WIDTH // LANES):
            q_lanes = slice(j * LANES, (j + 1) * LANES)
            k_lanes = slice(GLA_QK_WIDTH + j * LANES, GLA_QK_WIDTH + (j + 1) * LANES)
            store(COL_GLA_Q + j * LANES, rope(gqk[:, q_lanes]) * GLA_KEY_DIM ** -0.5)
            store(COL_GLA_K + j * LANES, rope(gqk[:, k_lanes]))

        store(COL_GLA_V, proj(COL_GLA_V, GLA_V_WIDTH))

    normalize(0)
    for s in range(sub_tiles):
        project(s, (lambda nxt=s + 1: normalize(nxt)) if s + 1 < sub_tiles else (lambda: None))


def _in_proj(x, mods, gain, w_all, w_a2, b_a2, rotate, tm, sub_tiles):
    b, l, d = x.shape
    per_batch = mods.shape[0] > 1
    cos, sin = _rope_tables(l, rotate)
    tok = lambda width: pl.BlockSpec((None, tm, width), lambda bi, i: (bi, i, 0))
    out_widths = [PROJ_WIDTH, 2 * GLA_QK_WIDTH]
    out_dtypes = [BF16, F32]
    return pl.pallas_call(
        functools.partial(_in_proj_kernel, sub_tiles=sub_tiles),
        grid=(b, l // tm),
        in_specs=[tok(d),
                  pl.BlockSpec((None,) + mods.shape[1:], lambda bi, i: (bi if per_batch else 0, 0, 0)),
                  _resident(gain.shape), _resident(w_all.shape),
                  _resident(w_a2.shape), _resident(b_a2.shape),
                  pl.BlockSpec((tm, LANES), lambda bi, i: (i, 0)),
                  pl.BlockSpec((tm, LANES), lambda bi, i: (i, 0))],
        out_specs=[tok(w) for w in out_widths],
        out_shape=[jax.ShapeDtypeStruct((b, l, w), dt) for w, dt in zip(out_widths, out_dtypes)],
        compiler_params=_params("parallel", "parallel"),
        name="in_proj",
    )(x, mods, gain, w_all, w_a2, b_a2, cos, sin)


def _gla_kernel(*refs, reverse, finalize, chunks, group_size):
    if finalize:
        q_ref, k_ref, lg_ref, v_ref, s0_ref, prev_ref, gr_ref, gain_ref, out_ref, sfin_ref, st_ref = refs
    else:
        q_ref, k_ref, lg_ref, v_ref, s0_ref, out_ref, sfin_ref, st_ref = refs
    step = pl.program_id(2)

    @pl.when(step == 0)
    def _():
        st_ref[...] = s0_ref[...]

    c = GLA_CHUNK
    row = lax.broadcasted_iota(jnp.int32, (c, c), 0)
    col = lax.broadcasted_iota(jnp.int32, (c, c), 1)
    tri = (row <= col) if reverse else (row >= col)
    tri_b = tri.astype(BF16)
    row2 = lax.broadcasted_iota(jnp.int32, (2 * c, 2 * c), 0) & (c - 1)
    col2 = lax.broadcasted_iota(jnp.int32, (2 * c, 2 * c), 1) & (c - 1)
    tri_pair = (row2 <= col2) if reverse else (row2 >= col2)
    lane = lax.broadcasted_iota(jnp.int32, (c, LANES), 1)
    first_head = lane < GLA_KEY_DIM
    end_row = 0 if reverse else c - 1

    def stack_heads(t):
        return jnp.concatenate([jnp.where(first_head, t, 0.0), jnp.where(first_head, 0.0, t)], axis=0).astype(BF16)

    heads = range(2)
    rows = [slice(i * c, (i + 1) * c) for i in range(chunks)]
    vals = [slice(h * GLA_VAL_DIM, (h + 1) * GLA_VAL_DIM) for h in heads]
    per_chunk = lambda: [None] * chunks
    bcum, q2, k2_inv, k2_end, decay = per_chunk(), per_chunk(), per_chunk(), per_chunk(), per_chunk()
    v2_t, scores, kv, attn, state_before = per_chunk(), per_chunk(), per_chunk(), per_chunk(), per_chunk()
    state = [st_ref[...]]

    def log_decay(group):
        halves = [_split_bf16(lg_ref[rows[i], :]) for i in group]
        hi = jnp.concatenate([h_[0] for h_ in halves], axis=1)
        lo = jnp.concatenate([h_[1] for h_ in halves], axis=1)
        both = _dot(tri_b, hi) + _dot(tri_b, lo)
        for n, i in enumerate(group):
            bcum[i] = both[:, n * LANES:(n + 1) * LANES]

    def decayed_operands(group):
        for i in group:
            b_end = bcum[i][end_row:end_row + 1, :]
            k = k_ref[rows[i], :].astype(F32)
            q2[i] = stack_heads(q_ref[rows[i], :].astype(F32) * jnp.exp(bcum[i]))
            k2_inv[i] = stack_heads(k * jnp.exp(-bcum[i]))
            k2_end[i] = stack_heads(k * jnp.exp(b_end - bcum[i]))
            decay[i] = jnp.exp(b_end)

    def chunk_products(group):
        for i in group:
            v2 = jnp.concatenate([v_ref[rows[i], vals[h]] for h in heads], axis=0)
            v2_t[i] = v2.T
            scores[i] = _dot_nt(q2[i], k2_inv[i])
            kv[i] = _dot(v2_t[i], k2_end[i])

    def mask_and_carry(group):
        for i in group:
            attn[i] = jnp.where(tri_pair, scores[i], 0.0).astype(BF16)
            state_before[i] = state[0].astype(BF16)
            state[0] = state[0] * decay[i] + kv[i]

    def outputs(group):
        for i in group:
            lhs = jnp.concatenate([q2[i], attn[i]], axis=1)
            rhs = jnp.concatenate([state_before[i], v2_t[i]], axis=1)
            o2 = _dot_nt(lhs, rhs)
            for h in heads:
                o = o2[h * c:(h + 1) * c, :]
                if finalize:
                    o = _rms(o + prev_ref[rows[i], vals[h]])
                    gate = gr_ref[rows[i], vals[h]].astype(F32)
                    out_ref[rows[i], vals[h]] = (o * gain_ref[:, vals[h]] * (gate * _sigmoid(gate))).astype(out_ref.dtype)
                else:
                    out_ref[rows[i], vals[h]] = o

    order = list(reversed(range(chunks))) if reverse else list(range(chunks))
    groups = [order[g:g + group_size] for g in range(0, chunks, group_size)]
    stages = [log_decay, decayed_operands, chunk_products, mask_and_carry, outputs]
    for tick in range(len(groups) + len(stages) - 1):
        for k, stage in enumerate(stages):
            if 0 <= tick - k < len(groups):
                stage(groups[tick - k])
    st_ref[...] = state[0]

    @pl.when(step == pl.num_programs(2) - 1)
    def _():
        sfin_ref[...] = st_ref[...]


def _gla_scan(proj, gates, s0, *, reverse, tb, prev=None, gain=None):
    b, l, _ = proj.shape
    finalize = prev is not None
    nb = l // tb
    pairs = GLA_HEADS // 2
    pair_vals = 2 * GLA_VAL_DIM
    blk = (lambda i: nb - 1 - i) if reverse else (lambda i: i)
    lanes_at = lambda col: pl.BlockSpec((None, tb, LANES), lambda bi, p, i: (bi, blk(i), col // LANES + p))
    vals_at = lambda col: pl.BlockSpec((None, tb, pair_vals), lambda bi, p, i: (bi, blk(i), col // pair_vals + p))
    st_spec = pl.BlockSpec((None, None, GLA_VAL_DIM, LANES), lambda bi, p, i: (bi, p, 0, 0))
    v_spec = vals_at(0)
    in_specs = [lanes_at(COL_GLA_Q), lanes_at(COL_GLA_K), lanes_at(GLA_QK_WIDTH if reverse else 0),
                vals_at(COL_GLA_V), st_spec]
    args = [proj, proj, gates, proj, s0]
    if finalize:
        in_specs += [v_spec, vals_at(COL_GR), pl.BlockSpec((1, pair_vals), lambda bi, p, i: (0, p))]
        args += [prev, proj, gain]
    chunks = tb // GLA_CHUNK
    kernel = functools.partial(_gla_kernel, reverse=reverse, finalize=finalize, chunks=chunks,
                               group_size=min(GLA_GROUP[reverse], chunks))
    return pl.pallas_call(
        kernel,
        grid=(b, pairs, nb),
        in_specs=in_specs,
        out_specs=[v_spec, st_spec],
        out_shape=[jax.ShapeDtypeStruct((b, l, GLA_V_WIDTH), BF16 if finalize else F32),
                   jax.ShapeDtypeStruct(s0.shape, F32)],
        scratch_shapes=[pltpu.VMEM((GLA_VAL_DIM, LANES), F32)],
        compiler_params=_params("parallel", "parallel", "arbitrary"),
        name="gla_scan_bwd" if reverse else "gla_scan_fwd",
    )(*args)


def _bias_kernel(rpb_ref, o_ref):
    kw, w = NA_WIN_COLS, GRID_W
    c = lax.broadcasted_iota(jnp.int32, (w, LANES), 0)
    lane = lax.broadcasted_iota(jnp.int32, (w, LANES), 1)
    kc = lane & (w - 1)
    start = jnp.clip(c - kw // 2, 0, w - kw)
    in_window = (kc >= start) & (kc < start + kw)
    second_row = lane >= w
    toeplitz = [pltpu.roll(jnp.broadcast_to(rpb_ref[r:r + 1, :], (w, LANES)), LANES - (w - 1), 1, stride=1, stride_axis=0)
                for r in range(rpb_ref.shape[0])]
    for e in range(o_ref.shape[0]):
        slab = jnp.where(second_row, pltpu.roll(toeplitz[e + 1], w, 1), toeplitz[e])
        o_ref[e] = jnp.where(in_window, slab * LOG2_E, MASK_VALUE)


def _na_bias_table(rpb):
    heads, n_rows, n_cols = rpb.shape
    lead = GRID_W - NA_WIN_COLS
    padded = jnp.pad(rpb, ((0, 0), (0, 0), (lead, LANES - lead - n_cols)))
    return pl.pallas_call(
        _bias_kernel,
        grid=(heads,),
        in_specs=[pl.BlockSpec((None, n_rows, LANES), lambda h: (h, 0, 0))],
        out_specs=pl.BlockSpec((None, n_rows - 1, GRID_W, LANES), lambda h: (h, 0, 0, 0)),
        out_shape=jax.ShapeDtypeStruct((heads, n_rows - 1, GRID_W, LANES), F32),
        compiler_params=_params("parallel"),
        name="na_bias_table",
    )(padded)


def _na_kernel(q_ref, k_ref, v_ref, kc_ref, vc_ref, bias_ref, o_ref, *, rows_per_step, group_size, grid_rows):
    kr, w = NA_WIN_ROWS, GRID_W
    band = kr * w
    lane = lax.broadcasted_iota(jnp.int32, (w, LANES), 1)
    first_head = lane < NA_HEAD_DIM
    base_row = pl.program_id(2) * rows_per_step
    per_row = lambda: [None] * rows_per_step
    band_rows, s_loc, s_ctx, p_loc, p_ctx, denom = per_row(), per_row(), per_row(), per_row(), per_row(), per_row()
    pair = [slice(n * 2 * w, (n + 1) * 2 * w) for n in range(group_size)]

    def logits(group):
        stacked = []
        for i in group:
            q = q_ref[i * w:(i + 1) * w, :]
            zero = jnp.zeros_like(q)
            stacked += [jnp.where(first_head, q, zero), jnp.where(first_head, zero, q)]
        q2 = jnp.concatenate(stacked, axis=0)
        s_ctx_group = _dot_nt(q2, kc_ref[...])
        for n, i in enumerate(group):
            r = base_row + i
            start = jnp.clip(r - kr // 2, 0, grid_rows - kr)
            band_rows[i] = pl.ds(pl.multiple_of(start * w, w), band)
            first_slab = kr - 1 - (r - start)
            bias = jnp.concatenate([jnp.concatenate([bias_ref[h, first_slab + 2 * j] for j in range(kr // 2)], axis=1)
                                    for h in range(2)], axis=0)
            s_loc[i] = _dot_nt(q2[pair[n], :], k_ref[band_rows[i], :]) + bias
            s_ctx[i] = s_ctx_group[pair[n], :]

    def softmax(group):
        for i in group:
            m = jnp.max(_fold_slabs(jnp.maximum, s_loc[i], s_ctx[i]), axis=-1, keepdims=True)
            p = jnp.exp2(s_loc[i] - m)
            p_c = jnp.exp2(s_ctx[i] - m)
            denom[i] = jnp.sum(_fold_slabs(jnp.add, p, p_c), axis=-1, keepdims=True)
            p_loc[i] = p.astype(BF16)
            p_ctx[i] = p_c.astype(BF16)

    def weighted_values(group):
        pv_ctx = _dot(jnp.concatenate([p_ctx[i] for i in group], axis=0), vc_ref[...])
        for n, i in enumerate(group):
            pv = (_dot(p_loc[i], v_ref[band_rows[i], :]) + pv_ctx[pair[n], :]) * (1.0 / denom[i])
            o_ref[i * w:(i + 1) * w, :] = jnp.where(first_head, pv[:w], pv[w:]).astype(o_ref.dtype)

    groups = [list(range(g, g + group_size)) for g in range(0, rows_per_step, group_size)]
    stages = [logits, softmax, weighted_values]
    for tick in range(len(groups) + len(stages) - 1):
        for k, stage in enumerate(stages):
            if 0 <= tick - k < len(groups):
                stage(groups[tick - k])


def _na_attention(proj, ctx_proj, bias, rows_per_step, group_size):
    b, l, _ = proj.shape
    n_ctx = ctx_proj.shape[1]
    grid_rows = l // GRID_W
    tq = rows_per_step * GRID_W
    pairs = NA_HEADS // 2
    q_spec = pl.BlockSpec((None, tq, LANES), lambda bi, p, i: (bi, i, COL_NA_Q // LANES + p))
    seq_at = lambda col: pl.BlockSpec((None, l, LANES), lambda bi, p, i: (bi, 0, col // LANES + p))
    ctx_at = lambda col: pl.BlockSpec((None, n_ctx, LANES), lambda bi, p, i: (bi, 0, col // LANES + p))
    out_spec = pl.BlockSpec((None, tq, LANES), lambda bi, p, i: (bi, i, p))
    bias_spec = pl.BlockSpec((2,) + bias.shape[1:], lambda bi, p, i: (p, 0, 0, 0))
    kernel = functools.partial(_na_kernel, rows_per_step=rows_per_step, group_size=min(group_size, rows_per_step),
                               grid_rows=grid_rows)
    return pl.pallas_call(
        kernel,
        grid=(b, pairs, grid_rows // rows_per_step),
        in_specs=[q_spec, seq_at(COL_NA_K), seq_at(COL_NA_V), ctx_at(COL_NA_K), ctx_at(COL_NA_V), bias_spec],
        out_specs=out_spec,
        out_shape=jax.ShapeDtypeStruct((b, l, NA_WIDTH), BF16),
        compiler_params=_params("parallel", "parallel", "arbitrary"),
        name="na_attention",
    )(proj, proj, proj, ctx_proj, ctx_proj, bias)


def _out_ffn_kernel(x_ref, na_ref, gla_ref, mod_ref, gains_ref, wo_ref, wgu_ref, wd_ref, o_ref, *,
                    hidden_chunks, sub_tiles):
    gate_mix, shift_ffn, scale_ffn, gate_ffn = (mod_ref[i:i + 1, :] for i in (2, 3, 4, 5))
    g_post_mix, g_pre_ffn, g_post_ffn = (gains_ref[i:i + 1, :] for i in (0, 1, 2))
    half = na_ref.shape[-1]
    hidden = wd_ref.shape[0]
    ts = x_ref.shape[0] // sub_tiles
    rows = [slice(s * ts, (s + 1) * ts) for s in range(sub_tiles)]
    x1, h, acc = [None] * sub_tiles, [None] * sub_tiles, [None] * sub_tiles

    y = [_dot(na_ref[r, :], wo_ref[:half, :]) + _dot(gla_ref[r, :], wo_ref[half:, :]) for r in rows]

    def mix(s):
        x1[s] = x_ref[rows[s], :] + gate_mix * (_rms(y[s]) * g_post_mix)
        h[s] = ((_rms(x1[s]) * g_pre_ffn) * (1.0 + scale_ffn) + shift_ffn).astype(BF16)

    def ffn_chunk(s, c0, width):
        g = _dot(h[s], wgu_ref[:, c0:c0 + width])
        u = _dot(h[s], wgu_ref[:, hidden + c0:hidden + c0 + width])
        a = ((g * _sigmoid(g)) * u).astype(BF16)
        part = _dot(a, wd_ref[c0:c0 + width, :])
        acc[s] = part if acc[s] is None else acc[s] + part

    def finish(s):
        o_ref[rows[s], :] = x1[s] + gate_ffn * (_rms(acc[s]) * g_post_ffn)

    mix(0)
    for s in range(sub_tiles):
        for ci, (c0, width) in enumerate(hidden_chunks):
            ffn_chunk(s, c0, width)
            if ci == 0 and s + 1 < sub_tiles:
                mix(s + 1)
            if ci == 0 and s > 0:
                finish(s - 1)
    finish(sub_tiles - 1)


def _out_ffn(x, na_out, gla_out, mods, gains, w_out, w_gate_up, w_down, tm, sub_tiles):
    b, l, d = x.shape
    hidden = w_down.shape[0]
    chunk = 1024
    hidden_chunks = tuple((c0, min(chunk, hidden - c0)) for c0 in range(0, hidden, chunk))
    tok = lambda width: pl.BlockSpec((None, tm, width), lambda bi, i: (bi, i, 0))
    kernel = functools.partial(_out_ffn_kernel, hidden_chunks=hidden_chunks, sub_tiles=sub_tiles)
    return pl.pallas_call(
        kernel,
        grid=(b, l // tm),
        in_specs=[tok(d), tok(na_out.shape[-1]), tok(gla_out.shape[-1]),
                  pl.BlockSpec((None,) + mods.shape[1:], lambda bi, i: (bi, 0, 0)),
                  _resident(gains.shape), _resident(w_out.shape), _resident(w_gate_up.shape),
                  _resident(w_down.shape)],
        out_specs=tok(d),
        out_shape=jax.ShapeDtypeStruct(x.shape, x.dtype),
        compiler_params=_params("parallel", "parallel"),
        name="out_ffn",
    )(x, na_out, gla_out, mods, gains, w_out, w_gate_up, w_down)


def kernel(x, c, ctx, c_ctx, w_mod, b_mod, norm_pre_mix, norm_post_mix, norm_pre_ffn, norm_post_ffn, w_in, na_rpb, gla_wa2_f, gla_ba_f, gla_wa2_b, gla_ba_b, gla_norm, w_out, w_gate_up, w_down):
    assert w_mod.shape[0] == 1, "single-layer stack: the context stream is only read, never updated"
    b, l, d = x.shape
    n_ctx = ctx.shape[1]
    rank = GLA_GATE_RANK

    cond = jnp.concatenate([c, c_ctx[None, :]], axis=0)
    cond = jnp.pad(cond, ((0, -cond.shape[0] % 8), (0, 0)))
    mods = _adaln_mod(cond, w_mod[0], b_mod[0][None, :])
    lat_mods = mods[:b].reshape(b, 6, d)
    ctx_mods = mods[b:b + 1].reshape(1, 6, d)

    main_width = w_in.shape[-1] - 2 * rank
    w_all = jnp.concatenate([w_in[0, :, :main_width].astype(BF16),
                             jnp.pad(w_in[0, :, main_width:].astype(BF16), ((0, 0), (0, LANES - 2 * rank)))], axis=1)
    w_a2 = jnp.zeros((LANES, 2 * GLA_QK_WIDTH), F32)
    w_a2 = w_a2.at[:rank, :GLA_QK_WIDTH].set(gla_wa2_f[0]).at[rank:2 * rank, GLA_QK_WIDTH:].set(gla_wa2_b[0])
    w_a2 = w_a2.astype(BF16)
    b_a2 = jnp.concatenate([gla_ba_f[0], gla_ba_b[0]])[None, :]

    proj = functools.partial(_in_proj, gain=norm_pre_mix, w_all=w_all, w_a2=w_a2, b_a2=b_a2)
    lat_proj, lat_gates = proj(x, lat_mods, rotate=True, tm=min(PROJ_TILE, l), sub_tiles=PROJ_SUB_TILES)
    ctx_proj, ctx_gates = [s.reshape(b, n_ctx, s.shape[-1]) for s in
                           proj(ctx.reshape(1, b * n_ctx, d), ctx_mods, rotate=False,
                                tm=min(CTX_PROJ_TILE, b * n_ctx), sub_tiles=1)]

    zero_state = jnp.zeros((b, GLA_HEADS // 2, GLA_VAL_DIM, LANES), F32)
    _, s_fwd = _gla_scan(ctx_proj, ctx_gates, zero_state, reverse=False, tb=min(GLA_BLOCK, n_ctx))
    _, s_bwd = _gla_scan(ctx_proj, ctx_gates, zero_state, reverse=True, tb=min(GLA_BLOCK, n_ctx))
    o_fwd, _ = _gla_scan(lat_proj, lat_gates, s_fwd, reverse=False, tb=min(GLA_BLOCK, l))
    gla_out, _ = _gla_scan(lat_proj, lat_gates, s_bwd, reverse=True, tb=min(GLA_BLOCK, l), prev=o_fwd, gain=gla_norm)

    na_out = _na_attention(lat_proj, ctx_proj, _na_bias_table(na_rpb[0]),
                           rows_per_step=min(NA_ROWS_PER_STEP, l // GRID_W), group_size=NA_GROUP)

    gains = jnp.concatenate([norm_post_mix, norm_pre_ffn, norm_post_ffn], axis=0)
    return _out_ffn(x, na_out, gla_out, lat_mods, gains, w_out[0].astype(BF16), w_gate_up[0].astype(BF16),
                    w_down[0].astype(BF16), tm=min(FFN_TILE, l), sub_tiles=FFN_SUB_TILES)
```

```python
import functools

import numpy as np
import jax
import jax.numpy as jnp
from jax import lax
from jax.experimental import pallas as pl
from jax.experimental.pallas import tpu as pltpu

F32 = jnp.float32
BF16 = jnp.bfloat16

GRID_W = 64
NA_HEADS = 8
NA_HEAD_DIM = 64
NA_WIN_ROWS = 8
NA_WIN_COLS = 16
GLA_HEADS = 4
GLA_KEY_DIM = 64
GLA_VAL_DIM = 128
GLA_GATE_RANK = 16
GLA_GATE_TAU = 16.0
GLA_CHUNK = 64
ROPE_BASE = 10000.0
NORM_EPS = 1e-6

NA_WIDTH = NA_HEADS * NA_HEAD_DIM
GLA_QK_WIDTH = GLA_HEADS * GLA_KEY_DIM
GLA_V_WIDTH = GLA_HEADS * GLA_VAL_DIM

COL_NA_Q, COL_NA_K, COL_NA_V = 0, NA_WIDTH, 2 * NA_WIDTH
COL_GLA_Q = 3 * NA_WIDTH
COL_GLA_K = COL_GLA_Q + GLA_QK_WIDTH
COL_GLA_V = COL_GLA_K + GLA_QK_WIDTH
COL_GR = COL_GLA_V + GLA_V_WIDTH
PROJ_WIDTH = COL_GR + GLA_V_WIDTH

LANES = 128
MASK_VALUE = -1e30
LOG2_E = 1.4426950408889634
NA_Q_SCALE = NA_HEAD_DIM ** -0.5 * LOG2_E
VMEM_LIMIT_BYTES = 56 * 1024 * 1024

MOD_TILE = 1024
PROJ_TILE, PROJ_SUB_TILES = 1024, 4
CTX_PROJ_TILE = 512
FFN_TILE, FFN_SUB_TILES = 1024, 4
GLA_BLOCK = 4096
GLA_GROUP = {False: 8, True: 2}
NA_ROWS_PER_STEP, NA_GROUP = 128, 1

NT_DIMS = (((1,), (1,)), ((), ()))


def _dot(a, b):
    return jnp.dot(a, b, preferred_element_type=F32)


def _dot_nt(a, b):
    return lax.dot_general(a, b, NT_DIMS, preferred_element_type=F32)


def _split_bf16(a):
    hi = a.astype(BF16)
    lo = (a - hi.astype(F32)).astype(BF16)
    return hi, lo


def _sigmoid(x):
    return 1.0 / (1.0 + jnp.exp(-x))


def _rms(x):
    return x * lax.rsqrt(jnp.mean(x * x, axis=-1, keepdims=True) + NORM_EPS)


def _fold_slabs(op, *arrays):
    slabs = [a[:, j:j + LANES] for a in arrays for j in range(0, a.shape[1], LANES)]
    return functools.reduce(op, slabs)


def _params(*semantics):
    return pltpu.CompilerParams(dimension_semantics=semantics, vmem_limit_bytes=VMEM_LIMIT_BYTES)


def _resident(shape):
    zeros = (0,) * len(shape)
    return pl.BlockSpec(shape, lambda *_: zeros)


def _mod_kernel(c_ref, w_ref, b_ref, o_ref):
    c = c_ref[...]
    s_hi, s_lo = _split_bf16(c * _sigmoid(c))
    w_hi, w_lo = _split_bf16(w_ref[...])
    o_ref[...] = _dot(s_hi, w_hi) + (_dot(s_hi, w_lo) + _dot(s_lo, w_hi)) + b_ref[...]


def _adaln_mod(cond, w_mod, b_mod):
    r, d = cond.shape
    n = w_mod.shape[1]
    tn = MOD_TILE
    return pl.pallas_call(
        _mod_kernel,
        grid=(n // tn,),
        in_specs=[pl.BlockSpec((r, d), lambda j: (0, 0)),
                  pl.BlockSpec((d, tn), lambda j: (0, j)),
                  pl.BlockSpec((1, tn), lambda j: (0, j))],
        out_specs=pl.BlockSpec((r, tn), lambda j: (0, j)),
        out_shape=jax.ShapeDtypeStruct((r, n), F32),
        compiler_params=_params("arbitrary"),
        name="adaln_mod",
    )(cond, w_mod, b_mod)


def _rope_tables(seq_len, rotate):
    if not rotate:
        return np.ones((seq_len, LANES), np.float32), np.zeros((seq_len, LANES), np.float32)
    quarter = GLA_KEY_DIM // 4
    t = np.arange(seq_len)
    inv = ROPE_BASE ** (-np.arange(quarter, dtype=np.float64) / quarter)
    ang_row = (t // GRID_W)[:, None] * inv[None, :]
    ang_col = (t % GRID_W)[:, None] * inv[None, :]
    cos = np.concatenate([np.cos(ang_row)] * 2 + [np.cos(ang_col)] * 2, axis=-1)
    sin = np.concatenate([-np.sin(ang_row), np.sin(ang_row), -np.sin(ang_col), np.sin(ang_col)], axis=-1)
    reps = LANES // GLA_KEY_DIM
    return np.tile(cos, (1, reps)).astype(np.float32), np.tile(sin, (1, reps)).astype(np.float32)


def _in_proj_kernel(x_ref, mod_ref, g_ref, w_ref, wa_ref, ba_ref, cos_ref, sin_ref, proj_ref, gate_ref, *, sub_tiles):
    shift, scale = mod_ref[0:1, :], mod_ref[1:2, :]
    ts = x_ref.shape[0] // sub_tiles
    quarter = GLA_KEY_DIM // 4
    lane = lax.broadcasted_iota(jnp.int32, (ts, LANES), 1)
    first_half = (lane & quarter) == 0
    hb = [None] * sub_tiles

    def normalize(s):
        x = x_ref[s * ts:(s + 1) * ts, :]
        hb[s] = ((_rms(x) * g_ref[...]) * (1.0 + scale) + shift).astype(BF16)

    def project(s, after_first_dot):
        rows = slice(s * ts, (s + 1) * ts)

        def proj(c0, width):
            return _dot(hb[s], w_ref[:, c0:c0 + width])

        def store(c0, value):
            proj_ref[rows, c0:c0 + value.shape[1]] = value.astype(BF16)

        gr_gate = proj(COL_GR, GLA_V_WIDTH + LANES)
        after_first_dot()
        store(COL_GR, gr_gate[:, :GLA_V_WIDTH])
        low_rank = gr_gate[:, GLA_V_WIDTH:].astype(BF16)
        z = _dot(low_rank, wa_ref[...]) + ba_ref[...]
        gate_ref[rows, :] = (jnp.minimum(z, 0.0) - jnp.log(1.0 + jnp.exp(-jnp.abs(z)))) * (1.0 / GLA_GATE_TAU)

        store(COL_NA_Q, proj(COL_NA_Q, NA_WIDTH) * NA_Q_SCALE)
        store(COL_NA_K, proj(COL_NA_K, NA_WIDTH))
        store(COL_NA_V, proj(COL_NA_V, NA_WIDTH))

        cos, sin = cos_ref[rows, :], sin_ref[rows, :]

        def rope(t):
            partner = jnp.where(first_half, pltpu.roll(t, LANES - quarter, 1), pltpu.roll(t, quarter, 1))
            return t * cos + partner * sin

        gqk = proj(COL_GLA_Q, 2 * GLA_QK_WIDTH)
        for j in range(GLA_QK_WIDTH // LANES):
            q_lanes = slice(j * LANES, (j + 1) * LANES)
            k_lanes = slice(GLA_QK_WIDTH + j * LANES, GLA_QK_WIDTH + (j + 1) * LANES)
            store(COL_GLA_Q + j * LANES, rope(gqk[:, q_lanes]) * GLA_KEY_DIM ** -0.5)
            store(COL_GLA_K + j * LANES, rope(gqk[:, k_lanes]))

        store(COL_GLA_V, proj(COL_GLA_V, GLA_V_WIDTH))

    normalize(0)
    for s in range(sub_tiles):
        project(s, (lambda nxt=s + 1: normalize(nxt)) if s + 1 < sub_tiles else (lambda: None))


def _in_proj(x, mods, gain, w_all, w_a2, b_a2, rotate, tm, sub_tiles):
    b, l, d = x.shape
    per_batch = mods.shape[0] > 1
    cos, sin = _rope_tables(l, rotate)
    tok = lambda width: pl.BlockSpec((None, tm, width), lambda bi, i: (bi, i, 0))
    out_widths = [PROJ_WIDTH, 2 * GLA_QK_WIDTH]
    out_dtypes = [BF16, F32]
    return pl.pallas_call(
        functools.partial(_in_proj_kernel, sub_tiles=sub_tiles),
        grid=(b, l // tm),
        in_specs=[tok(d),
                  pl.BlockSpec((None,) + mods.shape[1:], lambda bi, i: (bi if per_batch else 0, 0, 0)),
                  _resident(gain.shape), _resident(w_all.shape),
                  _resident(w_a2.shape), _resident(b_a2.shape),
                  pl.BlockSpec((tm, LANES), lambda bi, i: (i, 0)),
                  pl.BlockSpec((tm, LANES), lambda bi, i: (i, 0))],
        out_specs=[tok(w) for w in out_widths],
        out_shape=[jax.ShapeDtypeStruct((b, l, w), dt) for w, dt in zip(out_widths, out_dtypes)],
        compiler_params=_params("parallel", "parallel"),
        name="in_proj",
    )(x, mods, gain, w_all, w_a2, b_a2, cos, sin)


def _gla_kernel(*refs, reverse, finalize, chunks, group_size):
    if finalize:
        q_ref, k_ref, lg_ref, v_ref, s0_ref, prev_ref, gr_ref, gain_ref, out_ref, sfin_ref, st_ref = refs
    else:
        q_ref, k_ref, lg_ref, v_ref, s0_ref, out_ref, sfin_ref, st_ref = refs
    step = pl.program_id(2)

    @pl.when(step == 0)
    def _():
        st_ref[...] = s0_ref[...]

    c = GLA_CHUNK
    row = lax.broadcasted_iota(jnp.int32, (c, c), 0)
    col = lax.broadcasted_iota(jnp.int32, (c, c), 1)
    tri = (row <= col) if reverse else (row >= col)
    tri_b = tri.astype(BF16)
    row2 = lax.broadcasted_iota(jnp.int32, (2 * c, 2 * c), 0) & (c - 1)
    col2 = lax.broadcasted_iota(jnp.int32, (2 * c, 2 * c), 1) & (c - 1)
    tri_pair = (row2 <= col2) if reverse else (row2 >= col2)
    lane = lax.broadcasted_iota(jnp.int32, (c, LANES), 1)
    first_head = lane < GLA_KEY_DIM
    end_row = 0 if reverse else c - 1

    def stack_heads(t):
        return jnp.concatenate([jnp.where(first_head, t, 0.0), jnp.where(first_head, 0.0, t)], axis=0).astype(BF16)

    heads = range(2)
    rows = [slice(i * c, (i + 1) * c) for i in range(chunks)]
    vals = [slice(h * GLA_VAL_DIM, (h + 1) * GLA_VAL_DIM) for h in heads]
    per_chunk = lambda: [None] * chunks
    bcum, q2, k2_inv, k2_end, decay = per_chunk(), per_chunk(), per_chunk(), per_chunk(), per_chunk()
    v2_t, scores, kv, attn, state_before = per_chunk(), per_chunk(), per_chunk(), per_chunk(), per_chunk()
    state = [st_ref[...]]

    def log_decay(group):
        halves = [_split_bf16(lg_ref[rows[i], :]) for i in group]
        hi = jnp.concatenate([h_[0] for h_ in halves], axis=1)
        lo = jnp.concatenate([h_[1] for h_ in halves], axis=1)
        both = _dot(tri_b, hi) + _dot(tri_b, lo)
        for n, i in enumerate(group):
            bcum[i] = both[:, n * LANES:(n + 1) * LANES]

    def decayed_operands(group):
        for i in group:
            b_end = bcum[i][end_row:end_row + 1, :]
            k = k_ref[rows[i], :].astype(F32)
            q2[i] = stack_heads(q_ref[rows[i], :].astype(F32) * jnp.exp(bcum[i]))
            k2_inv[i] = stack_heads(k * jnp.exp(-bcum[i]))
            k2_end[i] = stack_heads(k * jnp.exp(b_end - bcum[i]))
            decay[i] = jnp.exp(b_end)

    def chunk_products(group):
        for i in group:
            v2 = jnp.concatenate([v_ref[rows[i], vals[h]] for h in heads], axis=0)
            v2_t[i] = v2.T
            scores[i] = _dot_nt(q2[i], k2_inv[i])
            kv[i] = _dot(v2_t[i], k2_end[i])

    def mask_and_carry(group):
        for i in group:
            attn[i] = jnp.where(tri_pair, scores[i], 0.0).astype(BF16)
            state_before[i] = state[0].astype(BF16)
            state[0] = state[0] * decay[i] + kv[i]

    def outputs(group):
        for i in group:
            lhs = jnp.concatenate([q2[i], attn[i]], axis=1)
            rhs = jnp.concatenate([state_before[i], v2_t[i]], axis=1)
            o2 = _dot_nt(lhs, rhs)
            for h in heads:
                o = o2[h * c:(h + 1) * c, :]
                if finalize:
                    o = _rms(o + prev_ref[rows[i], vals[h]])
                    gate = gr_ref[rows[i], vals[h]].astype(F32)
                    out_ref[rows[i], vals[h]] = (o * gain_ref[:, vals[h]] * (gate * _sigmoid(gate))).astype(out_ref.dtype)
                else:
                    out_ref[rows[i], vals[h]] = o

    order = list(reversed(range(chunks))) if reverse else list(range(chunks))
    groups = [order[g:g + group_size] for g in range(0, chunks, group_size)]
    stages = [log_decay, decayed_operands, chunk_products, mask_and_carry, outputs]
    for tick in range(len(groups) + len(stages) - 1):
        for k, stage in enumerate(stages):
            if 0 <= tick - k < len(groups):
                stage(groups[tick - k])
    st_ref[...] = state[0]

    @pl.when(step == pl.num_programs(2) - 1)
    def _():
        sfin_ref[...] = st_ref[...]


def _gla_scan(proj, gates, s0, *, reverse, tb, prev=None, gain=None):
    b, l, _ = proj.shape
    finalize = prev is not None
    nb = l // tb
    pairs = GLA_HEADS // 2
    pair_vals = 2 * GLA_VAL_DIM
    blk = (lambda i: nb - 1 - i) if reverse else (lambda i: i)
    lanes_at = lambda col: pl.BlockSpec((None, tb, LANES), lambda bi, p, i: (bi, blk(i), col // LANES + p))
    vals_at = lambda col: pl.BlockSpec((None, tb, pair_vals), lambda bi, p, i: (bi, blk(i), col // pair_vals + p))
    st_spec = pl.BlockSpec((None, None, GLA_VAL_DIM, LANES), lambda bi, p, i: (bi, p, 0, 0))
    v_spec = vals_at(0)
    in_specs = [lanes_at(COL_GLA_Q), lanes_at(COL_GLA_K), lanes_at(GLA_QK_WIDTH if reverse else 0),
                vals_at(COL_GLA_V), st_spec]
    args = [proj, proj, gates, proj, s0]
    if finalize:
        in_specs += [v_spec, vals_at(COL_GR), pl.BlockSpec((1, pair_vals), lambda bi, p, i: (0, p))]
        args += [prev, proj, gain]
    chunks = tb // GLA_CHUNK
    kernel = functools.partial(_gla_kernel, reverse=reverse, finalize=finalize, chunks=chunks,
                               group_size=min(GLA_GROUP[reverse], chunks))
    return pl.pallas_call(
        kernel,
        grid=(b, pairs, nb),
        in_specs=in_specs,
        out_specs=[v_spec, st_spec],
        out_shape=[jax.ShapeDtypeStruct((b, l, GLA_V_WIDTH), BF16 if finalize else F32),
                   jax.ShapeDtypeStruct(s0.shape, F32)],
        scratch_shapes=[pltpu.VMEM((GLA_VAL_DIM, LANES), F32)],
        compiler_params=_params("parallel", "parallel", "arbitrary"),
        name="gla_scan_bwd" if reverse else "gla_scan_fwd",
    )(*args)


def _bias_kernel(rpb_ref, o_ref):
    kw, w = NA_WIN_COLS, GRID_W
    c = lax.broadcasted_iota(jnp.int32, (w, LANES), 0)
    lane = lax.broadcasted_iota(jnp.int32, (w, LANES), 1)
    kc = lane & (w - 1)
    start = jnp.clip(c - kw // 2, 0, w - kw)
    in_window = (kc >= start) & (kc < start + kw)
    second_row = lane >= w
    toeplitz = [pltpu.roll(jnp.broadcast_to(rpb_ref[r:r + 1, :], (w, LANES)), LANES - (w - 1), 1, stride=1, stride_axis=0)
                for r in range(rpb_ref.shape[0])]
    for e in range(o_ref.shape[0]):
        slab = jnp.where(second_row, pltpu.roll(toeplitz[e + 1], w, 1), toeplitz[e])
        o_ref[e] = jnp.where(in_window, slab * LOG2_E, MASK_VALUE)


def _na_bias_table(rpb):
    heads, n_rows, n_cols = rpb.shape
    lead = GRID_W - NA_WIN_COLS
    padded = jnp.pad(rpb, ((0, 0), (0, 0), (lead, LANES - lead - n_cols)))
    return pl.pallas_call(
        _bias_kernel,
        grid=(heads,),
        in_specs=[pl.BlockSpec((None, n_rows, LANES), lambda h: (h, 0, 0))],
        out_specs=pl.BlockSpec((None, n_rows - 1, GRID_W, LANES), lambda h: (h, 0, 0, 0)),
        out_shape=jax.ShapeDtypeStruct((heads, n_rows - 1, GRID_W, LANES), F32),
        compiler_params=_params("parallel"),
        name="na_bias_table",
    )(padded)


def _na_kernel(q_ref, k_ref, v_ref, kc_ref, vc_ref, bias_ref, o_ref, *, rows_per_step, group_size, grid_rows):
    kr, w = NA_WIN_ROWS, GRID_W
    band = kr * w
    lane = lax.broadcasted_iota(jnp.int32, (w, LANES), 1)
    first_head = lane < NA_HEAD_DIM
    base_row = pl.program_id(2) * rows_per_step
    per_row = lambda: [None] * rows_per_step
    band_rows, s_loc, s_ctx, p_loc, p_ctx, denom = per_row(), per_row(), per_row(), per_row(), per_row(), per_row()
    pair = [slice(n * 2 * w, (n + 1) * 2 * w) for n in range(group_size)]

    def logits(group):
        stacked = []
        for i in group:
            q = q_ref[i * w:(i + 1) * w, :]
            zero = jnp.zeros_like(q)
            stacked += [jnp.where(first_head, q, zero), jnp.where(first_head, zero, q)]
        q2 = jnp.concatenate(stacked, axis=0)
        s_ctx_group = _dot_nt(q2, kc_ref[...])
        for n, i in enumerate(group):
            r = base_row + i
            start = jnp.clip(r - kr // 2, 0, grid_rows - kr)
            band_rows[i] = pl.ds(pl.multiple_of(start * w, w), band)
            first_slab = kr - 1 - (r - start)
            bias = jnp.concatenate([jnp.concatenate([bias_ref[h, first_slab + 2 * j] for j in range(kr // 2)], axis=1)
                                    for h in range(2)], axis=0)
            s_loc[i] = _dot_nt(q2[pair[n], :], k_ref[band_rows[i], :]) + bias
            s_ctx[i] = s_ctx_group[pair[n], :]

    def softmax(group):
        for i in group:
            m = jnp.max(_fold_slabs(jnp.maximum, s_loc[i], s_ctx[i]), axis=-1, keepdims=True)
            p = jnp.exp2(s_loc[i] - m)
            p_c = jnp.exp2(s_ctx[i] - m)
            denom[i] = jnp.sum(_fold_slabs(jnp.add, p, p_c), axis=-1, keepdims=True)
            p_loc[i] = p.astype(BF16)
            p_ctx[i] = p_c.astype(BF16)

    def weighted_values(group):
        pv_ctx = _dot(jnp.concatenate([p_ctx[i] for i in group], axis=0), vc_ref[...])
        for n, i in enumerate(group):
            pv = (_dot(p_loc[i], v_ref[band_rows[i], :]) + pv_ctx[pair[n], :]) * (1.0 / denom[i])
            o_ref[i * w:(i + 1) * w, :] = jnp.where(first_head, pv[:w], pv[w:]).astype(o_ref.dtype)

    groups = [list(range(g, g + group_size)) for g in range(0, rows_per_step, group_size)]
    stages = [logits, softmax, weighted_values]
    for tick in range(len(groups) + len(stages) - 1):
        for k, stage in enumerate(stages):
            if 0 <= tick - k < len(groups):
                stage(groups[tick - k])


def _na_attention(proj, ctx_proj, bias, rows_per_step, group_size):
    b, l, _ = proj.shape
    n_ctx = ctx_proj.shape[1]
    grid_rows = l // GRID_W
    tq = rows_per_step * GRID_W
    pairs = NA_HEADS // 2
    q_spec = pl.BlockSpec((None, tq, LANES), lambda bi, p, i: (bi, i, COL_NA_Q // LANES + p))
    seq_at = lambda col: pl.BlockSpec((None, l, LANES), lambda bi, p, i: (bi, 0, col // LANES + p))
    ctx_at = lambda col: pl.BlockSpec((None, n_ctx, LANES), lambda bi, p, i: (bi, 0, col // LANES + p))
    out_spec = pl.BlockSpec((None, tq, LANES), lambda bi, p, i: (bi, i, p))
    bias_spec = pl.BlockSpec((2,) + bias.shape[1:], lambda bi, p, i: (p, 0, 0, 0))
    kernel = functools.partial(_na_kernel, rows_per_step=rows_per_step, group_size=min(group_size, rows_per_step),
                               grid_rows=grid_rows)
    return pl.pallas_call(
        kernel,
        grid=(b, pairs, grid_rows // rows_per_step),
        in_specs=[q_spec, seq_at(COL_NA_K), seq_at(COL_NA_V), ctx_at(COL_NA_K), ctx_at(COL_NA_V), bias_spec],
        out_specs=out_spec,
        out_shape=jax.ShapeDtypeStruct((b, l, NA_WIDTH), BF16),
        compiler_params=_params("parallel", "parallel", "arbitrary"),
        name="na_attention",
    )(proj, proj, proj, ctx_proj, ctx_proj, bias)


def _out_ffn_kernel(x_ref, na_ref, gla_ref, mod_ref, gains_ref, wo_ref, wgu_ref, wd_ref, o_ref, *,
                    hidden_chunks, sub_tiles):
    gate_mix, shift_ffn, scale_ffn, gate_ffn = (mod_ref[i:i + 1, :] for i in (2, 3, 4, 5))
    g_post_mix, g_pre_ffn, g_post_ffn = (gains_ref[i:i + 1, :] for i in (0, 1, 2))
    half = na_ref.shape[-1]
    hidden = wd_ref.shape[0]
    ts = x_ref.shape[0] // sub_tiles
    rows = [slice(s * ts, (s + 1) * ts) for s in range(sub_tiles)]
    x1, h, acc = [None] * sub_tiles, [None] * sub_tiles, [None] * sub_tiles

    y = [_dot(na_ref[r, :], wo_ref[:half, :]) + _dot(gla_ref[r, :], wo_ref[half:, :]) for r in rows]

    def mix(s):
        x1[s] = x_ref[rows[s], :] + gate_mix * (_rms(y[s]) * g_post_mix)
        h[s] = ((_rms(x1[s]) * g_pre_ffn) * (1.0 + scale_ffn) + shift_ffn).astype(BF16)

    def ffn_chunk(s, c0, width):
        g = _dot(h[s], wgu_ref[:, c0:c0 + width])
        u = _dot(h[s], wgu_ref[:, hidden + c0:hidden + c0 + width])
        a = ((g * _sigmoid(g)) * u).astype(BF16)
        part = _dot(a, wd_ref[c0:c0 + width, :])
        acc[s] = part if acc[s] is None else acc[s] + part

    def finish(s):
        o_ref[rows[s], :] = x1[s] + gate_ffn * (_rms(acc[s]) * g_post_ffn)

    mix(0)
    for s in range(sub_tiles):
        for ci, (c0, width) in enumerate(hidden_chunks):
            ffn_chunk(s, c0, width)
            if ci == 0 and s + 1 < sub_tiles:
                mix(s + 1)
            if ci == 0 and s > 0:
                finish(s - 1)
    finish(sub_tiles - 1)


def _out_ffn(x, na_out, gla_out, mods, gains, w_out, w_gate_up, w_down, tm, sub_tiles):
    b, l, d = x.shape
    hidden = w_down.shape[0]
    chunk = 1024
    hidden_chunks = tuple((c0, min(chunk, hidden - c0)) for c0 in range(0, hidden, chunk))
    tok = lambda width: pl.BlockSpec((None, tm, width), lambda bi, i: (bi, i, 0))
    kernel = functools.partial(_out_ffn_kernel, hidden_chunks=hidden_chunks, sub_tiles=sub_tiles)
    return pl.pallas_call(
        kernel,
        grid=(b, l // tm),
        in_specs=[tok(d), tok(na_out.shape[-1]), tok(gla_out.shape[-1]),
                  pl.BlockSpec((None,) + mods.shape[1:], lambda bi, i: (bi, 0, 0)),
                  _resident(gains.shape), _resident(w_out.shape), _resident(w_gate_up.shape),
                  _resident(w_down.shape)],
        out_specs=tok(d),
        out_shape=jax.ShapeDtypeStruct(x.shape, x.dtype),
        compiler_params=_params("parallel", "parallel"),
        name="out_ffn",
    )(x, na_out, gla_out, mods, gains, w_out, w_gate_up, w_down)


def kernel(x, c, ctx, c_ctx, w_mod, b_mod, norm_pre_mix, norm_post_mix, norm_pre_ffn, norm_post_ffn, w_in, na_rpb, gla_wa2_f, gla_ba_f, gla_wa2_b, gla_ba_b, gla_norm, w_out, w_gate_up, w_down):
    assert w_mod.shape[0] == 1, "single-layer stack: the context stream is only read, never updated"
    b, l, d = x.shape
    n_ctx = ctx.shape[1]
    rank = GLA_GATE_RANK

    cond = jnp.concatenate([c, c_ctx[None, :]], axis=0)
    cond = jnp.pad(cond, ((0, -cond.shape[0] % 8), (0, 0)))
    mods = _adaln_mod(cond, w_mod[0], b_mod[0][None, :])
    lat_mods = mods[:b].reshape(b, 6, d)
    ctx_mods = mods[b:b + 1].reshape(1, 6, d)

    main_width = w_in.shape[-1] - 2 * rank
    w_all = jnp.concatenate([w_in[0, :, :main_width].astype(BF16),
                             jnp.pad(w_in[0, :, main_width:].astype(BF16), ((0, 0), (0, LANES - 2 * rank)))], axis=1)
    w_a2 = jnp.zeros((LANES, 2 * GLA_QK_WIDTH), F32)
    w_a2 = w_a2.at[:rank, :GLA_QK_WIDTH].set(gla_wa2_f[0]).at[rank:2 * rank, GLA_QK_WIDTH:].set(gla_wa2_b[0])
    w_a2 = w_a2.astype(BF16)
    b_a2 = jnp.concatenate([gla_ba_f[0], gla_ba_b[0]])[None, :]

    proj = functools.partial(_in_proj, gain=norm_pre_mix, w_all=w_all, w_a2=w_a2, b_a2=b_a2)
    lat_proj, lat_gates = proj(x, lat_mods, rotate=True, tm=min(PROJ_TILE, l), sub_tiles=PROJ_SUB_TILES)
    ctx_proj, ctx_gates = [s.reshape(b, n_ctx, s.shape[-1]) for s in
                           proj(ctx.reshape(1, b * n_ctx, d), ctx_mods, rotate=False,
                                tm=min(CTX_PROJ_TILE, b * n_ctx), sub_tiles=1)]

    zero_state = jnp.zeros((b, GLA_HEADS // 2, GLA_VAL_DIM, LANES), F32)
    _, s_fwd = _gla_scan(ctx_proj, ctx_gates, zero_state, reverse=False, tb=min(GLA_BLOCK, n_ctx))
    _, s_bwd = _gla_scan(ctx_proj, ctx_gates, zero_state, reverse=True, tb=min(GLA_BLOCK, n_ctx))
    o_fwd, _ = _gla_scan(lat_proj, lat_gates, s_fwd, reverse=False, tb=min(GLA_BLOCK, l))
    gla_out, _ = _gla_scan(lat_proj, lat_gates, s_bwd, reverse=True, tb=min(GLA_BLOCK, l), prev=o_fwd, gain=gla_norm)

    na_out = _na_attention(lat_proj, ctx_proj, _na_bias_table(na_rpb[0]),
                           rows_per_step=min(NA_ROWS_PER_STEP, l // GRID_W), group_size=NA_GROUP)

    gains = jnp.concatenate([norm_post_mix, norm_pre_ffn, norm_post_ffn], axis=0)
    return _out_ffn(x, na_out, gla_out, lat_mods, gains, w_out[0].astype(BF16), w_gate_up[0].astype(BF16),
                    w_down[0].astype(BF16), tm=min(FFN_TILE, l), sub_tiles=FFN_SUB_TILES)
```

```python
import functools

import numpy as np
import jax
import jax.numpy as jnp
from jax import lax
from jax.experimental import pallas as pl
from jax.experimental.pallas import tpu as pltpu

F32 = jnp.float32
BF16 = jnp.bfloat16

GRID_W = 64
NA_HEADS = 8
NA_HEAD_DIM = 64
NA_WIN_ROWS = 8
NA_WIN_COLS = 16
GLA_HEADS = 4
GLA_KEY_DIM = 64
GLA_VAL_DIM = 128
GLA_GATE_RANK = 16
GLA_GATE_TAU = 16.0
GLA_CHUNK = 64
ROPE_BASE = 10000.0
NORM_EPS = 1e-6

NA_WIDTH = NA_HEADS * NA_HEAD_DIM
GLA_QK_WIDTH = GLA_HEADS * GLA_KEY_DIM
GLA_V_WIDTH = GLA_HEADS * GLA_VAL_DIM

COL_NA_Q, COL_NA_K, COL_NA_V = 0, NA_WIDTH, 2 * NA_WIDTH
COL_GLA_Q = 3 * NA_WIDTH
COL_GLA_K = COL_GLA_Q + GLA_QK_WIDTH
COL_GLA_V = COL_GLA_K + GLA_QK_WIDTH
COL_GR = COL_GLA_V + GLA_V_WIDTH
PROJ_WIDTH = COL_GR + GLA_V_WIDTH

LANES = 128
MASK_VALUE = -1e30
LOG2_E = 1.4426950408889634
NA_Q_SCALE = NA_HEAD_DIM ** -0.5 * LOG2_E
VMEM_LIMIT_BYTES = 56 * 1024 * 1024

PROJ_TILE, PROJ_SUB_TILES = 1024, 4
CTX_PROJ_TILE = 512
FFN_TILE, FFN_SUB_TILES = 1024, 4
GLA_BLOCK = 4096
GLA_GROUP = {False: 8, True: 2}
NA_ROWS_PER_STEP, NA_GROUP = 128, 1

NT_DIMS = (((1,), (1,)), ((), ()))
TN_DIMS = (((0,), (0,)), ((), ()))


def _dot(a, b):
    return jnp.dot(a, b, preferred_element_type=F32)


def _dot_nt(a, b):
    return lax.dot_general(a, b, NT_DIMS, preferred_element_type=F32)


def _dot_tn(a, b):
    return lax.dot_general(a, b, TN_DIMS, preferred_element_type=F32)


def _split_bf16(a):
    hi = a.astype(BF16)
    lo = (a - hi.astype(F32)).astype(BF16)
    return hi, lo


def _sigmoid(x):
    return 1.0 / (1.0 + jnp.exp(-x))


def _rms(x):
    return x * lax.rsqrt(jnp.mean(x * x, axis=-1, keepdims=True) + NORM_EPS)


def _fold_slabs(op, *arrays):
    slabs = [a[:, j:j + LANES] for a in arrays for j in range(0, a.shape[1], LANES)]
    return functools.reduce(op, slabs)


def _params(*semantics):
    return pltpu.CompilerParams(dimension_semantics=semantics, vmem_limit_bytes=VMEM_LIMIT_BYTES)


def _resident(shape):
    zeros = (0,) * len(shape)
    return pl.BlockSpec(shape, lambda *_: zeros)


def _mod_kernel(c_ref, w_ref, b_ref, o_ref):
    c = c_ref[...]
    s_hi, s_lo = _split_bf16(c * _sigmoid(c))
    w_hi, w_lo = _split_bf16(w_ref[...])
    o_ref[...] = _dot(s_hi, w_hi) + (_dot(s_hi, w_lo) + _dot(s_lo, w_hi)) + b_ref[...]


def _adaln_mod(cond, w_mod, b_mod):
    r, d = cond.shape
    n = w_mod.shape[1]
    tn = 1024
    return pl.pallas_call(
        _mod_kernel,
        grid=(n // tn,),
        in_specs=[pl.BlockSpec((r, d), lambda j: (0, 0)),
                  pl.BlockSpec((d, tn), lambda j: (0, j)),
                  pl.BlockSpec((1, tn), lambda j: (0, j))],
        out_specs=pl.BlockSpec((r, tn), lambda j: (0, j)),
        out_shape=jax.ShapeDtypeStruct((r, n), F32),
        compiler_params=_params("arbitrary"),
        name="adaln_mod",
    )(cond, w_mod, b_mod)


def _rope_tables(seq_len, rotate):
    if not rotate:
        return np.ones((seq_len, LANES), np.float32), np.zeros((seq_len, LANES), np.float32)
    quarter = GLA_KEY_DIM // 4
    t = np.arange(seq_len)
    inv = ROPE_BASE ** (-np.arange(quarter, dtype=np.float64) / quarter)
    ang_row = (t // GRID_W)[:, None] * inv[None, :]
    ang_col = (t % GRID_W)[:, None] * inv[None, :]
    cos = np.concatenate([np.cos(ang_row)] * 2 + [np.cos(ang_col)] * 2, axis=-1)
    sin = np.concatenate([-np.sin(ang_row), np.sin(ang_row), -np.sin(ang_col), np.sin(ang_col)], axis=-1)
    reps = LANES // GLA_KEY_DIM
    return np.tile(cos, (1, reps)).astype(np.float32), np.tile(sin, (1, reps)).astype(np.float32)


def _in_proj_kernel(x_ref, mod_ref, g_ref, w_ref, wa_ref, ba_ref, cos_ref, sin_ref, proj_ref, gate_ref, *, sub_tiles):
    shift, scale = mod_ref[0:1, :], mod_ref[1:2, :]
    ts = x_ref.shape[0] // sub_tiles
    quarter = GLA_KEY_DIM // 4
    lane = lax.broadcasted_iota(jnp.int32, (ts, LANES), 1)
    first_half = (lane & quarter) == 0
    hb = [None] * sub_tiles

    def normalize(s):
        x = x_ref[s * ts:(s + 1) * ts, :]
        hb[s] = ((_rms(x) * g_ref[...]) * (1.0 + scale) + shift).astype(BF16)

    def project(s, after_first_dot):
        rows = slice(s * ts, (s + 1) * ts)

        def proj(c0, width):
            return _dot(hb[s], w_ref[:, c0:c0 + width])

        def store(c0, value):
            proj_ref[rows, c0:c0 + value.shape[1]] = value.astype(BF16)

        gr_gate = proj(COL_GR, GLA_V_WIDTH + LANES)
        after_first_dot()
        store(COL_GR, gr_gate[:, :GLA_V_WIDTH])
        low_rank = gr_gate[:, GLA_V_WIDTH:].astype(BF16)
        z = _dot(low_rank, wa_ref[...]) + ba_ref[...]
        gate_ref[rows, :] = (jnp.minimum(z, 0.0) - jnp.log(1.0 + jnp.exp(-jnp.abs(z)))) * (1.0 / GLA_GATE_TAU)

        store(COL_NA_Q, proj(COL_NA_Q, NA_WIDTH) * NA_Q_SCALE)
        store(COL_NA_K, proj(COL_NA_K, NA_WIDTH))
        store(COL_NA_V, proj(COL_NA_V, NA_WIDTH))

        cos, sin = cos_ref[rows, :], sin_ref[rows, :]

        def rope(t):
            partner = jnp.where(first_half, pltpu.roll(t, LANES - quarter, 1), pltpu.roll(t, quarter, 1))
            return t * cos + partner * sin

        gqk = proj(COL_GLA_Q, 2 * GLA_QK_WIDTH)
        for j in range(GLA_QK_WIDTH // LANES):
            q_lanes = slice(j * LANES, (j + 1) * LANES)
            k_lanes = slice(GLA_QK_WIDTH + j * LANES, GLA_QK_WIDTH + (j + 1) * LANES)
            store(COL_GLA_Q + j * LANES, rope(gqk[:, q_lanes]) * GLA_KEY_DIM ** -0.5)
            store(COL_GLA_K + j * LANES, rope(gqk[:, k_lanes]))

        store(COL_GLA_V, proj(COL_GLA_V, GLA_V_WIDTH))

    normalize(0)
    for s in range(sub_tiles):
        project(s, (lambda nxt=s + 1: normalize(nxt)) if s + 1 < sub_tiles else (lambda: None))


def _in_proj(x, mods, gain, w_all, w_a2, b_a2, rotate, tm, sub_tiles):
    b, l, d = x.shape
    per_batch = mods.shape[0] > 1
    cos, sin = _rope_tables(l, rotate)
    tok = lambda width: pl.BlockSpec((None, tm, width), lambda bi, i: (bi, i, 0))
    out_widths = [PROJ_WIDTH, 2 * GLA_QK_WIDTH]
    out_dtypes = [BF16, F32]
    return pl.pallas_call(
        functools.partial(_in_proj_kernel, sub_tiles=sub_tiles),
        grid=(b, l // tm),
        in_specs=[tok(d),
                  pl.BlockSpec((None,) + mods.shape[1:], lambda bi, i: (bi if per_batch else 0, 0, 0)),
                  _resident(gain.shape), _resident(w_all.shape),
                  _resident(w_a2.shape), _resident(b_a2.shape),
                  pl.BlockSpec((tm, LANES), lambda bi, i: (i, 0)),
                  pl.BlockSpec((tm, LANES), lambda bi, i: (i, 0))],
        out_specs=[tok(w) for w in out_widths],
        out_shape=[jax.ShapeDtypeStruct((b, l, w), dt) for w, dt in zip(out_widths, out_dtypes)],
        compiler_params=_params("parallel", "parallel"),
        name="in_proj",
    )(x, mods, gain, w_all, w_a2, b_a2, cos, sin)


def _gla_kernel(*refs, reverse, finalize, chunks, group_size):
    if finalize:
        q_ref, k_ref, lg_ref, v_ref, s0_ref, prev_ref, out_ref, sfin_ref, st_ref = refs
    else:
        q_ref, k_ref, lg_ref, v_ref, s0_ref, out_ref, sfin_ref, st_ref = refs
    step = pl.program_id(2)

    @pl.when(step == 0)
    def _():
        st_ref[...] = s0_ref[...]

    c = GLA_CHUNK
    row = lax.broadcasted_iota(jnp.int32, (c, c), 0)
    col = lax.broadcasted_iota(jnp.int32, (c, c), 1)
    tri = (row <= col) if reverse else (row >= col)
    tri_b = tri.astype(BF16)
    row2 = lax.broadcasted_iota(jnp.int32, (2 * c, 2 * c), 0) & (c - 1)
    col2 = lax.broadcasted_iota(jnp.int32, (2 * c, 2 * c), 1) & (c - 1)
    tri_pair = (row2 <= col2) if reverse else (row2 >= col2)
    lane = lax.broadcasted_iota(jnp.int32, (c, LANES), 1)
    first_head = lane < GLA_KEY_DIM
    end_row = 0 if reverse else c - 1

    def stack_heads(t):
        return jnp.concatenate([jnp.where(first_head, t, 0.0), jnp.where(first_head, 0.0, t)], axis=0).astype(BF16)

    heads = range(2)
    rows = [slice(i * c, (i + 1) * c) for i in range(chunks)]
    vals = [slice(h * GLA_VAL_DIM, (h + 1) * GLA_VAL_DIM) for h in heads]
    per_chunk = lambda: [None] * chunks
    bcum, q2, k2_inv, k2_end, decay = per_chunk(), per_chunk(), per_chunk(), per_chunk(), per_chunk()
    v2_t, scores, kv, attn, state_before = per_chunk(), per_chunk(), per_chunk(), per_chunk(), per_chunk()
    state = [st_ref[...]]

    def log_decay(group):
        halves = [_split_bf16(lg_ref[rows[i], :]) for i in group]
        hi = jnp.concatenate([h_[0] for h_ in halves], axis=1)
        lo = jnp.concatenate([h_[1] for h_ in halves], axis=1)
        both = _dot(tri_b, hi) + _dot(tri_b, lo)
        for n, i in enumerate(group):
            bcum[i] = both[:, n * LANES:(n + 1) * LANES]

    def decayed_operands(group):
        for i in group:
            b_end = bcum[i][end_row:end_row + 1, :]
            k = k_ref[rows[i], :].astype(F32)
            q2[i] = stack_heads(q_ref[rows[i], :].astype(F32) * jnp.exp(bcum[i]))
            k2_inv[i] = stack_heads(k * jnp.exp(-bcum[i]))
            k2_end[i] = stack_heads(k * jnp.exp(b_end - bcum[i]))
            decay[i] = jnp.exp(b_end)

    def chunk_products(group):
        for i in group:
            v2 = jnp.concatenate([v_ref[rows[i], vals[h]] for h in heads], axis=0)
            v2_t[i] = v2.T
            scores[i] = _dot_nt(q2[i], k2_inv[i])
            kv[i] = _dot(v2_t[i], k2_end[i])

    def mask_and_carry(group):
        for i in group:
            attn[i] = jnp.where(tri_pair, scores[i], 0.0).astype(BF16)
            state_before[i] = state[0].astype(BF16)
            state[0] = state[0] * decay[i] + kv[i]

    def outputs(group):
        for i in group:
            lhs = jnp.concatenate([q2[i], attn[i]], axis=1)
            rhs = jnp.concatenate([state_before[i], v2_t[i]], axis=1)
            o2 = _dot_nt(lhs, rhs)
            for h in heads:
                o = o2[h * c:(h + 1) * c, :]
                out_ref[rows[i], vals[h]] = o + prev_ref[rows[i], vals[h]] if finalize else o

    order = list(reversed(range(chunks))) if reverse else list(range(chunks))
    groups = [order[g:g + group_size] for g in range(0, chunks, group_size)]
    stages = [log_decay, decayed_operands, chunk_products, mask_and_carry, outputs]
    for tick in range(len(groups) + len(stages) - 1):
        for k, stage in enumerate(stages):
            if 0 <= tick - k < len(groups):
                stage(groups[tick - k])
    st_ref[...] = state[0]

    @pl.when(step == pl.num_programs(2) - 1)
    def _():
        sfin_ref[...] = st_ref[...]


def _gla_scan(proj, gates, s0, *, reverse, tb, prev=None):
    b, l, _ = proj.shape
    finalize = prev is not None
    nb = l // tb
    pairs = GLA_HEADS // 2
    pair_vals = 2 * GLA_VAL_DIM
    blk = (lambda i: nb - 1 - i) if reverse else (lambda i: i)
    lanes_at = lambda col: pl.BlockSpec((None, tb, LANES), lambda bi, p, i: (bi, blk(i), col // LANES + p))
    vals_at = lambda col: pl.BlockSpec((None, tb, pair_vals), lambda bi, p, i: (bi, blk(i), col // pair_vals + p))
    st_spec = pl.BlockSpec((None, None, GLA_VAL_DIM, LANES), lambda bi, p, i: (bi, p, 0, 0))
    v_spec = vals_at(0)
    in_specs = [lanes_at(COL_GLA_Q), lanes_at(COL_GLA_K), lanes_at(GLA_QK_WIDTH if reverse else 0),
                vals_at(COL_GLA_V), st_spec]
    args = [proj, proj, gates, proj, s0]
    if finalize:
        in_specs += [v_spec]
        args += [prev]
    chunks = tb // GLA_CHUNK
    kernel = functools.partial(_gla_kernel, reverse=reverse, finalize=finalize, chunks=chunks,
                               group_size=min(GLA_GROUP[reverse], chunks))
    return pl.pallas_call(
        kernel,
        grid=(b, pairs, nb),
        in_specs=in_specs,
        out_specs=[v_spec, st_spec],
        out_shape=[jax.ShapeDtypeStruct((b, l, GLA_V_WIDTH), F32),
                   jax.ShapeDtypeStruct(s0.shape, F32)],
        scratch_shapes=[pltpu.VMEM((GLA_VAL_DIM, LANES), F32)],
        compiler_params=_params("parallel", "parallel", "arbitrary"),
        name="gla_scan_bwd" if reverse else "gla_scan_fwd",
    )(*args)


def _bias_kernel(rpb_ref, o_ref):
    kw, w = NA_WIN_COLS, GRID_W
    c = lax.broadcasted_iota(jnp.int32, (w, LANES), 0)
    lane = lax.broadcasted_iota(jnp.int32, (w, LANES), 1)
    kc = lane & (w - 1)
    start = jnp.clip(c - kw // 2, 0, w - kw)
    in_window = (kc >= start) & (kc < start + kw)
    second_row = lane >= w
    toeplitz = [pltpu.roll(jnp.broadcast_to(rpb_ref[r:r + 1, :], (w, LANES)), LANES - (w - 1), 1, stride=1, stride_axis=0)
                for r in range(rpb_ref.shape[0])]
    for e in range(o_ref.shape[0]):
        slab = jnp.where(second_row, pltpu.roll(toeplitz[e + 1], w, 1), toeplitz[e])
        o_ref[e] = jnp.where(in_window, slab * LOG2_E, MASK_VALUE)


def _na_bias_table(rpb):
    heads, n_rows, n_cols = rpb.shape
    lead = GRID_W - NA_WIN_COLS
    padded = jnp.pad(rpb, ((0, 0), (0, 0), (lead, LANES - lead - n_cols)))
    return pl.pallas_call(
        _bias_kernel,
        grid=(heads,),
        in_specs=[pl.BlockSpec((None, n_rows, LANES), lambda h: (h, 0, 0))],
        out_specs=pl.BlockSpec((None, n_rows - 1, GRID_W, LANES), lambda h: (h, 0, 0, 0)),
        out_shape=jax.ShapeDtypeStruct((heads, n_rows - 1, GRID_W, LANES), F32),
        compiler_params=_params("parallel"),
        name="na_bias_table",
    )(padded)


def _na_kernel(q_ref, k_ref, v_ref, kc_ref, vc_ref, bias_ref, o_ref, *, rows_per_step, group_size, grid_rows):
    kr, w = NA_WIN_ROWS, GRID_W
    band = kr * w
    lane = lax.broadcasted_iota(jnp.int32, (w, LANES), 1)
    first_head = lane < NA_HEAD_DIM
    base_row = pl.program_id(2) * rows_per_step
    per_row = lambda: [None] * rows_per_step
    band_rows, s_loc, s_ctx, p_loc, p_ctx, denom = per_row(), per_row(), per_row(), per_row(), per_row(), per_row()
    pair = [slice(n * 2 * w, (n + 1) * 2 * w) for n in range(group_size)]

    def logits(group):
        stacked = []
        for i in group:
            q = q_ref[i * w:(i + 1) * w, :]
            zero = jnp.zeros_like(q)
            stacked += [jnp.where(first_head, q, zero), jnp.where(first_head, zero, q)]
        q2 = jnp.concatenate(stacked, axis=0)
        s_ctx_group = _dot_nt(q2, kc_ref[...])
        for n, i in enumerate(group):
            r = base_row + i
            start = jnp.clip(r - kr // 2, 0, grid_rows - kr)
            band_rows[i] = pl.ds(pl.multiple_of(start * w, w), band)
            first_slab = kr - 1 - (r - start)
            bias = jnp.concatenate([jnp.concatenate([bias_ref[h, first_slab + 2 * j] for j in range(kr // 2)], axis=1)
                                    for h in range(2)], axis=0)
            s_loc[i] = _dot_nt(q2[pair[n], :], k_ref[band_rows[i], :]) + bias
            s_ctx[i] = s_ctx_group[pair[n], :]

    def softmax(group):
        for i in group:
            m = jnp.max(_fold_slabs(jnp.maximum, s_loc[i], s_ctx[i]), axis=-1, keepdims=True)
            p = jnp.exp2(s_loc[i] - m)
            p_c = jnp.exp2(s_ctx[i] - m)
            denom[i] = jnp.sum(_fold_slabs(jnp.add, p, p_c), axis=-1, keepdims=True)
            p_loc[i] = p.astype(BF16)
            p_ctx[i] = p_c.astype(BF16)

    def weighted_values(group):
        pv_ctx = _dot(jnp.concatenate([p_ctx[i] for i in group], axis=0), vc_ref[...])
        for n, i in enumerate(group):
            pv = (_dot(p_loc[i], v_ref[band_rows[i], :]) + pv_ctx[pair[n], :]) * (1.0 / denom[i])
            o_ref[i * w:(i + 1) * w, :] = jnp.where(first_head, pv[:w], pv[w:]).astype(o_ref.dtype)

    groups = [list(range(g, g + group_size)) for g in range(0, rows_per_step, group_size)]
    stages = [logits, softmax, weighted_values]
    for tick in range(len(groups) + len(stages) - 1):
        for k, stage in enumerate(stages):
            if 0 <= tick - k < len(groups):
                stage(groups[tick - k])


def _na_attention(proj, ctx_proj, bias, rows_per_step, group_size):
    b, l, _ = proj.shape
    n_ctx = ctx_proj.shape[1]
    grid_rows = l // GRID_W
    tq = rows_per_step * GRID_W
    pairs = NA_HEADS // 2
    q_spec = pl.BlockSpec((None, tq, LANES), lambda bi, p, i: (bi, i, COL_NA_Q // LANES + p))
    seq_at = lambda col: pl.BlockSpec((None, l, LANES), lambda bi, p, i: (bi, 0, col // LANES + p))
    ctx_at = lambda col: pl.BlockSpec((None, n_ctx, LANES), lambda bi, p, i: (bi, 0, col // LANES + p))
    out_spec = pl.BlockSpec((None, tq, LANES), lambda bi, p, i: (bi, i, p))
    bias_spec = pl.BlockSpec((2,) + bias.shape[1:], lambda bi, p, i: (p, 0, 0, 0))
    kernel = functools.partial(_na_kernel, rows_per_step=rows_per_step, group_size=min(group_size, rows_per_step),
                               grid_rows=grid_rows)
    return pl.pallas_call(
        kernel,
        grid=(b, pairs, grid_rows // rows_per_step),
        in_specs=[q_spec, seq_at(COL_NA_K), seq_at(COL_NA_V), ctx_at(COL_NA_K), ctx_at(COL_NA_V), bias_spec],
        out_specs=out_spec,
        out_shape=jax.ShapeDtypeStruct((b, l, NA_WIDTH), BF16),
        compiler_params=_params("parallel", "parallel", "arbitrary"),
        name="na_attention",
    )(proj, proj, proj, ctx_proj, ctx_proj, bias)


def _out_ffn_kernel(x_ref, na_ref, gla_o_ref, gr_ref, gla_gain_ref, mod_ref, gains_ref, wo_ref, wgu_ref, wd_ref,
                    o_ref, *, hidden_chunks, sub_tiles):
    gate_mix, shift_ffn, scale_ffn, gate_ffn = (mod_ref[i:i + 1, :] for i in (2, 3, 4, 5))
    g_post_mix, g_pre_ffn, g_post_ffn = (gains_ref[i:i + 1, :] for i in (0, 1, 2))
    half = na_ref.shape[-1]
    hidden = wd_ref.shape[0]
    ts = x_ref.shape[0] // sub_tiles
    rows = [slice(s * ts, (s + 1) * ts) for s in range(sub_tiles)]
    x1, h, acc = [None] * sub_tiles, [None] * sub_tiles, [None] * sub_tiles

    def gla_out(r):
        parts = []
        for c0 in range(0, GLA_V_WIDTH, GLA_VAL_DIM):
            head = slice(c0, c0 + GLA_VAL_DIM)
            gate = gr_ref[r, head].astype(F32)
            parts.append((_rms(gla_o_ref[r, head]) * gla_gain_ref[:, head] * (gate * _sigmoid(gate))).astype(BF16))
        return jnp.concatenate(parts, axis=1)

    y = [_dot(na_ref[r, :], wo_ref[:half, :]) + _dot(gla_out(r), wo_ref[half:, :]) for r in rows]

    def mix(s):
        x1[s] = x_ref[rows[s], :] + gate_mix * (_rms(y[s]) * g_post_mix)
        h[s] = ((_rms(x1[s]) * g_pre_ffn) * (1.0 + scale_ffn) + shift_ffn).astype(BF16)

    def ffn_chunk(s, c0, width):
        g = _dot(h[s], wgu_ref[:, c0:c0 + width])
        u = _dot(h[s], wgu_ref[:, hidden + c0:hidden + c0 + width])
        a = ((g * _sigmoid(g)) * u).astype(BF16)
        part = _dot(a, wd_ref[c0:c0 + width, :])
        acc[s] = part if acc[s] is None else acc[s] + part

    def finish(s):
        o_ref[rows[s], :] = x1[s] + gate_ffn * (_rms(acc[s]) * g_post_ffn)

    mix(0)
    for s in range(sub_tiles):
        for ci, (c0, width) in enumerate(hidden_chunks):
            ffn_chunk(s, c0, width)
            if ci == 0 and s + 1 < sub_tiles:
                mix(s + 1)
            if ci == 0 and s > 0:
                finish(s - 1)
    finish(sub_tiles - 1)


def _out_ffn(x, na_out, gla_sum, proj, gla_gain, mods, gains, w_out, w_gate_up, w_down, tm, sub_tiles):
    b, l, d = x.shape
    hidden = w_down.shape[0]
    chunk = 1024
    hidden_chunks = tuple((c0, min(chunk, hidden - c0)) for c0 in range(0, hidden, chunk))
    tok = lambda width: pl.BlockSpec((None, tm, width), lambda bi, i: (bi, i, 0))
    kernel = functools.partial(_out_ffn_kernel, hidden_chunks=hidden_chunks, sub_tiles=sub_tiles)
    return pl.pallas_call(
        kernel,
        grid=(b, l // tm),
        in_specs=[tok(d), tok(na_out.shape[-1]), tok(GLA_V_WIDTH),
                  pl.BlockSpec((None, tm, GLA_V_WIDTH), lambda bi, i: (bi, i, COL_GR // GLA_V_WIDTH)),
                  _resident(gla_gain.shape),
                  pl.BlockSpec((None,) + mods.shape[1:], lambda bi, i: (bi, 0, 0)),
                  _resident(gains.shape), _resident(w_out.shape), _resident(w_gate_up.shape),
                  _resident(w_down.shape)],
        out_specs=tok(d),
        out_shape=jax.ShapeDtypeStruct(x.shape, x.dtype),
        compiler_params=_params("parallel", "parallel"),
        name="out_ffn",
    )(x, na_out, gla_sum, proj, gla_gain, mods, gains, w_out, w_gate_up, w_down)


def kernel(x, c, ctx, c_ctx, w_mod, b_mod, norm_pre_mix, norm_post_mix, norm_pre_ffn, norm_post_ffn, w_in, na_rpb, gla_wa2_f, gla_ba_f, gla_wa2_b, gla_ba_b, gla_norm, w_out, w_gate_up, w_down):
    assert w_mod.shape[0] == 1, "single-layer stack: the context stream is only read, never updated"
    b, l, d = x.shape
    n_ctx = ctx.shape[1]
    rank = GLA_GATE_RANK

    cond = jnp.concatenate([c, c_ctx[None, :]], axis=0)
    cond = jnp.pad(cond, ((0, -cond.shape[0] % 8), (0, 0)))
    mods = _adaln_mod(cond, w_mod[0], b_mod[0][None, :])
    lat_mods = mods[:b].reshape(b, 6, d)
    ctx_mods = mods[b:b + 1].reshape(1, 6, d)

    main_width = w_in.shape[-1] - 2 * rank
    w_all = jnp.concatenate([w_in[0, :, :main_width].astype(BF16),
                             jnp.pad(w_in[0, :, main_width:].astype(BF16), ((0, 0), (0, LANES - 2 * rank)))], axis=1)
    w_a2 = jnp.zeros((LANES, 2 * GLA_QK_WIDTH), F32)
    w_a2 = w_a2.at[:rank, :GLA_QK_WIDTH].set(gla_wa2_f[0]).at[rank:2 * rank, GLA_QK_WIDTH:].set(gla_wa2_b[0])
    w_a2 = w_a2.astype(BF16)
    b_a2 = jnp.concatenate([gla_ba_f[0], gla_ba_b[0]])[None, :]

    proj = functools.partial(_in_proj, gain=norm_pre_mix, w_all=w_all, w_a2=w_a2, b_a2=b_a2)
    lat_proj, lat_gates = proj(x, lat_mods, rotate=True, tm=min(PROJ_TILE, l), sub_tiles=PROJ_SUB_TILES)
    ctx_proj, ctx_gates = [s.reshape(b, n_ctx, s.shape[-1]) for s in
                           proj(ctx.reshape(1, b * n_ctx, d), ctx_mods, rotate=False,
                                tm=min(CTX_PROJ_TILE, b * n_ctx), sub_tiles=1)]

    zero_state = jnp.zeros((b, GLA_HEADS // 2, GLA_VAL_DIM, LANES), F32)
    _, s_fwd = _gla_scan(ctx_proj, ctx_gates, zero_state, reverse=False, tb=min(GLA_BLOCK, n_ctx))
    _, s_bwd = _gla_scan(ctx_proj, ctx_gates, zero_state, reverse=True, tb=min(GLA_BLOCK, n_ctx))
    o_fwd, _ = _gla_scan(lat_proj, lat_gates, s_fwd, reverse=False, tb=min(GLA_BLOCK, l))
    gla_sum, _ = _gla_scan(lat_proj, lat_gates, s_bwd, reverse=True, tb=min(GLA_BLOCK, l), prev=o_fwd)

    na_out = _na_attention(lat_proj, ctx_proj, _na_bias_table(na_rpb[0]),
                           rows_per_step=min(NA_ROWS_PER_STEP, l // GRID_W), group_size=NA_GROUP)

    gains = jnp.concatenate([norm_post_mix, norm_pre_ffn, norm_post_ffn], axis=0)
    return _out_ffn(x, na_out, gla_sum, lat_proj, gla_norm, lat_mods, gains, w_out[0].astype(BF16), w_gate_up[0].astype(BF16),
                    w_down[0].astype(BF16), tm=min(FFN_TILE, l), sub_tiles=FFN_SUB_TILES)
```
